```python
import jax, jax.numpy as jnp
from jax import lax
import numpy as np

D_MODEL = 2048
BATCH = 4
SEQ = 8192
DEPTH = 1

GRID_W = 64
CTX_LEN = 256
NH_A = 8
DK_A = 128
DV_A = 256
QK_A = NH_A * DK_A
V_A = NH_A * DV_A
CONV_W = 3
CHUNK = 64
M_INIT = -1e30
NH_B = 16
NKV_B = 4
HD_B = 128
Q_B = NH_B * HD_B
KV_B = NKV_B * HD_B
ROT_HALF = HD_B // 2
ROPE_THETA = 10000.0
Q_BLOCK = 128
EPS = 1e-6
ALPHA = (2 * DEPTH) ** 0.25
BETA = (8 * DEPTH) ** -0.25
KV_WIDTHS = (2 * QK_A, V_A, 4 * NH_A, KV_B, KV_B)
N_KV = 2 * QK_A + V_A + 4 * NH_A + 2 * KV_B
OUT_WIDTHS = (V_A, V_A, Q_B, Q_B, 2 * D_MODEL)
N_IN = N_KV + 2 * V_A + 2 * Q_B + 2 * D_MODEL

kernel_name = "hybrid_mlstm_gqa_dit_block"


def _split(p, widths, start=0):
    outs = []
    off = start
    for w in widths:
        outs.append(p[..., off:off + w])
        off += w
    return outs


def layer_norm(x, w=None, b=None):
    xf = x.astype(jnp.float32)
    mu = xf.mean(-1, keepdims=True)
    var = jnp.mean(jnp.square(xf - mu), -1, keepdims=True)
    y = (xf - mu) * lax.rsqrt(var + EPS)
    if w is not None:
        y = y * w.astype(jnp.float32) + b.astype(jnp.float32)
    return y.astype(x.dtype)


def rms_norm(x, w):
    xf = x.astype(jnp.float32)
    y = xf * lax.rsqrt(jnp.mean(jnp.square(xf), -1, keepdims=True) + EPS)
    return (y * w.astype(jnp.float32)).astype(x.dtype)


def dwconv_centred(x, w, b):
    T = x.shape[1]
    pad = CONV_W // 2
    xp = jnp.pad(x, ((0, 0), (pad, CONV_W - 1 - pad), (0, 0)))
    y = b
    for j in range(CONV_W):
        y = y + xp[:, j:j + T] * w[j]
    return y


def rope_tables(n):
    rows_n = n // GRID_W
    row = jnp.repeat(jnp.arange(rows_n), GRID_W).astype(jnp.float32)
    col = jnp.tile(jnp.arange(GRID_W), rows_n).astype(jnp.float32)
    inv = ROPE_THETA ** (-jnp.arange(0, ROT_HALF, 2, dtype=jnp.float32) / ROT_HALF)
    ang_r = row[:, None] * inv[None]
    ang_c = col[:, None] * inv[None]
    return (jnp.cos(ang_r), jnp.sin(ang_r), jnp.cos(ang_c), jnp.sin(ang_c))


def _rot(xh, cos, sin):
    h = xh.shape[-1] // 2
    x1, x2 = xh[..., :h], xh[..., h:]
    cos = cos[None, :, None, :]
    sin = sin[None, :, None, :]
    return jnp.concatenate([x1 * cos - x2 * sin, x1 * sin + x2 * cos], axis=-1)


def apply_rope_2d(x, rope):
    cr, sr, cc, sc = rope
    xf = x.astype(jnp.float32)
    y = jnp.concatenate([_rot(xf[..., :ROT_HALF], cr, sr),
                         _rot(xf[..., ROT_HALF:], cc, sc)], axis=-1)
    return y.astype(x.dtype)


def zero_state(b):
    return (jnp.zeros((b, NH_A, DV_A, DK_A), jnp.float32),
            jnp.zeros((b, NH_A, DK_A), jnp.float32),
            jnp.full((b, NH_A), M_INIT, jnp.float32))


def mlstm_chunked(q, k, v, log_i, log_f, state):
    B, T, H, _ = q.shape
    nc = T // CHUNK

    def to_chunks(a):
        a = a.reshape((B, nc, CHUNK, H) + a.shape[3:])
        return jnp.moveaxis(a, (1, 3), (0, 2))

    tril = jnp.tril(jnp.ones((CHUNK, CHUNK), bool))

    def step(carry, xs):
        C0, n0, m0 = carry
        qc, kc, vc, ic, fc = xs
        b = jnp.cumsum(fc, axis=-1)
        d = jnp.where(tril, b[..., :, None] - b[..., None, :] + ic[..., None, :], -jnp.inf)
        m_inter = b + m0[..., None]
        m = jnp.maximum(m_inter, d.max(-1))
        w = jnp.exp(d - m[..., None])
        a = jnp.exp(m_inter - m)
        s = jnp.einsum('bhjd,bhsd->bhjs', qc, kc) * w
        num = (a[..., None] * jnp.einsum('bhvd,bhjd->bhjv', C0, qc)
               + jnp.einsum('bhjs,bhsv->bhjv', s, vc))
        den = a * jnp.einsum('bhd,bhjd->bhj', n0, qc) + s.sum(-1)
        h = num / jnp.maximum(jnp.abs(den), jnp.exp(-m))[..., None]
        m_end = m[..., -1]
        w_end = jnp.exp(b[..., -1:] - b + ic - m_end[..., None])
        a_end = a[..., -1]
        C = a_end[..., None, None] * C0 + jnp.einsum('bhs,bhsv,bhsd->bhvd', w_end, vc, kc)
        n = a_end[..., None] * n0 + jnp.einsum('bhs,bhsd->bhd', w_end, kc)
        return (C, n, m_end), h

    xs = tuple(to_chunks(a) for a in (q, k, v, log_i, log_f))
    state, h = lax.scan(step, state, xs)
    h = jnp.moveaxis(h, (0, 2), (1, 3)).reshape(B, T, H, v.shape[-1])
    return h, state


def mlstm_final_state(k, v, log_i, log_f):
    b = jnp.cumsum(log_f, axis=1)
    g = b[:, -1:] - b + log_i
    m = g.max(axis=1)
    w = jnp.exp(g - m[:, None])
    C = jnp.einsum('bth,bthv,bthd->bhvd', w, v, k)
    n = jnp.einsum('bth,bthd->bhd', w, k)
    return (C, n, m)


def _flip(*arrs):
    return [jnp.flip(a, axis=1) for a in arrs]


def mlstm_inputs(qk_pre, v_a, if_a, conv_w, conv_b, b_if):
    B, T = v_a.shape[:2]
    qk = jax.nn.silu(dwconv_centred(qk_pre, conv_w, conv_b)).astype(jnp.float32)
    q = qk[..., :QK_A].reshape(B, T, NH_A, DK_A)
    k = qk[..., QK_A:].reshape(B, T, NH_A, DK_A) * (DK_A ** -0.5)
    v = v_a.astype(jnp.float32).reshape(B, T, NH_A, DV_A)
    gt = (if_a + b_if).astype(jnp.float32).reshape(B, T, 4, NH_A)
    fwd = (gt[:, :, 0], jax.nn.log_sigmoid(gt[:, :, 1]))
    bwd = (gt[:, :, 2], jax.nn.log_sigmoid(gt[:, :, 3]))
    return q, k, v, fwd, bwd


def attn_kv(k_b, v_b, k_norm_w, rope):
    B, T = k_b.shape[:2]
    k = rms_norm(k_b.reshape(B, T, NKV_B, HD_B), k_norm_w)
    if rope is not None:
        k = apply_rope_2d(k, rope)
    return k, v_b.reshape(B, T, NKV_B, HD_B)


def attend_blocks(q, k_all, v_all):
    B, S = q.shape[:2]
    G = NH_B // NKV_B
    nb = S // Q_BLOCK
    qb = q.reshape(B, nb, Q_BLOCK, NKV_B, G, HD_B).transpose(1, 0, 2, 3, 4, 5)
    scale = HD_B ** -0.5

    def one(qblk):
        s = jnp.einsum('bqkgd,btkd->bkgqt', qblk, k_all).astype(jnp.float32) * scale
        p = jax.nn.softmax(s, axis=-1).astype(v_all.dtype)
        return jnp.einsum('bkgqt,btkd->bqkgd', p, v_all)

    o = lax.map(one, qb)
    return o.transpose(1, 0, 2, 3, 4, 5).reshape(B, S, Q_B)


def branch_merge(h_a, o_attn, o_a, z_a, z_b, g_logits, mh_norm_w, w_ba, w_bb, w_out):
    B, T = h_a.shape[:2]
    h = rms_norm(h_a, mh_norm_w.reshape(NH_A, DV_A)).reshape(B, T, V_A)
    y_a = (jax.nn.sigmoid(o_a) * h * jax.nn.silu(z_a)) @ w_ba
    y_b = (o_attn * jax.nn.silu(z_b)) @ w_bb
    g_a, g_b = jnp.split(g_logits, 2, axis=-1)
    return (jax.nn.sigmoid(g_a) * y_a + jax.nn.sigmoid(g_b) * y_b) @ w_out


def trunk_layer(x, ctx, c, c_ctx, rope, w_mod, b_mod, w_in, b_if, conv_w, conv_b,
                mh_norm_w, q_norm_w, k_norm_w, w_ba, w_bb, w_out, ln_w, ln_b, update_ctx):
    B, S = x.shape[:2]
    T_c = ctx.shape[1]
    shift, scale, gate = [m[:, None, :] for m in jnp.split(jax.nn.silu(c) @ w_mod + b_mod, 3, axis=-1)]
    shift_c, scale_c, gate_c = jnp.split(jax.nn.silu(c_ctx) @ w_mod + b_mod, 3, axis=-1)
    u = layer_norm(x) * (1 + scale) + shift
    u_c = layer_norm(ctx) * (1 + scale_c) + shift_c
    p = u @ w_in
    p_c = u_c @ (w_in if update_ctx else w_in[:, :N_KV])

    qk_c, va_c, if_c, kb_c, vb_c = _split(p_c, KV_WIDTHS)
    qc, kc, vc, gf_c, gb_c = mlstm_inputs(qk_c, va_c, if_c, conv_w, conv_b, b_if)
    k_bc, v_bc = attn_kv(kb_c, vb_c, k_norm_w, None)
    if update_ctx:
        h_cf, st_f = mlstm_chunked(qc, kc, vc, *gf_c, zero_state(B))
        h_cb, st_b = mlstm_chunked(*_flip(qc, kc, vc, *gb_c), zero_state(B))
        h_c = (h_cf + jnp.flip(h_cb, axis=1)).astype(ctx.dtype)
        o_ac, z_ac, q_bc, z_bc, g_c = _split(p_c, OUT_WIDTHS, N_KV)
        q_bc = rms_norm(q_bc.reshape(B, T_c, NH_B, HD_B), q_norm_w)
        o_attn_c = attend_blocks(q_bc, k_bc, v_bc)
        out_c = branch_merge(h_c, o_attn_c, o_ac, z_ac, z_bc, g_c, mh_norm_w, w_ba, w_bb, w_out)
        ctx_new = layer_norm(ALPHA * ctx + gate_c * out_c, ln_w, ln_b)
    else:
        st_f = mlstm_final_state(kc, vc, *gf_c)
        st_b = mlstm_final_state(*_flip(kc, vc, *gb_c))
        ctx_new = ctx

    qk_l, va_l, if_l, kb_l, vb_l = _split(p, KV_WIDTHS)
    o_a, z_a, q_b, z_b, g_l = _split(p, OUT_WIDTHS, N_KV)
    ql, kl, vl, gf_l, gb_l = mlstm_inputs(qk_l, va_l, if_l, conv_w, conv_b, b_if)
    h_f, _ = mlstm_chunked(ql, kl, vl, *gf_l, st_f)
    h_b, _ = mlstm_chunked(*_flip(ql, kl, vl, *gb_l), st_b)
    h_l = (h_f + jnp.flip(h_b, axis=1)).astype(x.dtype)

    q_l = apply_rope_2d(rms_norm(q_b.reshape(B, S, NH_B, HD_B), q_norm_w), rope)
    k_bl, v_bl = attn_kv(kb_l, vb_l, k_norm_w, rope)
    k_all = jnp.concatenate([k_bc, k_bl], axis=1)
    v_all = jnp.concatenate([v_bc, v_bl], axis=1)
    o_attn = attend_blocks(q_l, k_all, v_all)

    out = branch_merge(h_l, o_attn, o_a, z_a, z_b, g_l, mh_norm_w, w_ba, w_bb, w_out)
    x_new = layer_norm(ALPHA * x + gate * out, ln_w, ln_b)
    return x_new, ctx_new


def setup_inputs(seed: int = 0) -> dict:
    key = jax.random.key(seed)
    ks = jax.random.split(key, 20)
    D = D_MODEL
    nrm = jax.random.normal
    b_if_i = 0.1 * nrm(ks[8], (DEPTH, 2, 1, NH_A))
    b_if_f = 3.0 + 0.5 * nrm(ks[9], (DEPTH, 2, 1, NH_A))
    b_if = jnp.concatenate([b_if_i, b_if_f], axis=2).reshape(DEPTH, 4 * NH_A)
    return {
        "x": nrm(ks[0], (BATCH, SEQ, D), jnp.float32),
        "c": nrm(ks[1], (BATCH, D), jnp.float32),
        "ctx": nrm(ks[2], (BATCH, CTX_LEN, D), jnp.float32),
        "c_ctx": nrm(ks[3], (D,), jnp.float32),
        "w_mod": 0.5 * D ** -0.5 * nrm(ks[4], (DEPTH, D, 3 * D), jnp.float32),
        "b_mod": 0.01 * nrm(ks[5], (DEPTH, 3 * D), jnp.float32),
        "w_in": D ** -0.5 * nrm(ks[6], (DEPTH, D, N_IN), jnp.float32),
        "b_if": b_if.astype(jnp.float32),
        "conv_w": CONV_W ** -0.5 * nrm(ks[7], (DEPTH, CONV_W, 2 * QK_A), jnp.float32),
        "conv_b": 0.01 * nrm(ks[10], (DEPTH, 2 * QK_A), jnp.float32),
        "mh_norm_w": 1.0 + 0.02 * nrm(ks[11], (DEPTH, V_A), jnp.float32),
        "q_norm_w": 1.0 + 0.02 * nrm(ks[12], (DEPTH, HD_B), jnp.float32),
        "k_norm_w": 1.0 + 0.02 * nrm(ks[13], (DEPTH, HD_B), jnp.float32),
        "w_branch_a": BETA * V_A ** -0.5 * nrm(ks[14], (DEPTH, V_A, D), jnp.float32),
        "w_branch_b": BETA * Q_B ** -0.5 * nrm(ks[15], (DEPTH, Q_B, D), jnp.float32),
        "w_out": BETA * D ** -0.5 * nrm(ks[16], (DEPTH, D, D), jnp.float32),
        "ln_w": 1.0 + 0.02 * nrm(ks[17], (DEPTH, D), jnp.float32),
        "ln_b": 0.01 * nrm(ks[18], (DEPTH, D), jnp.float32),
    }


def reference(x, c, ctx, c_ctx, w_mod, b_mod, w_in, b_if, conv_w, conv_b, mh_norm_w,
              q_norm_w, k_norm_w, w_branch_a, w_branch_b, w_out, ln_w, ln_b):
    rope = rope_tables(x.shape[1])
    for layer in range(DEPTH):
        x, ctx = trunk_layer(x, ctx, c, c_ctx, rope, w_mod[layer], b_mod[layer], w_in[layer],
                             b_if[layer], conv_w[layer], conv_b[layer], mh_norm_w[layer],
                             q_norm_w[layer], k_norm_w[layer], w_branch_a[layer],
                             w_branch_b[layer], w_out[layer], ln_w[layer], ln_b[layer],
                             layer < DEPTH - 1)
    return x
```

```python
import functools

import jax
import jax.numpy as jnp
from jax import lax
from jax.experimental import pallas as pl
from jax.experimental.pallas import tpu as pltpu

F32 = jnp.float32
BF16 = jnp.bfloat16

D_MODEL = 2048
GRID_W = 64
NH_A, DK_A, DV_A = 8, 128, 256
QK_A, V_A = NH_A * DK_A, NH_A * DV_A
N_GATES = 4 * NH_A
NH_B, NKV_B, HD_B = 16, 4, 128
GROUP_B = NH_B // NKV_B
Q_B, KV_B = NH_B * HD_B, NKV_B * HD_B
ROT_HALF = HD_B // 2
ROPE_THETA = 10000.0
EPS = 1e-6
DEPTH = 1
ALPHA = (2 * DEPTH) ** 0.25
M_INIT = -1e30

LANES = 128
BF16_SUBLANES = 16
VMEM_LIMIT = 56 * 1024 * 1024

TOK = 256

P_QK, P_VA, P_OA, P_ZA, P_QB, P_ZB, P_GA, P_GB = (i * D_MODEL for i in range(8))
P_KB = 8 * D_MODEL
P_VB = P_KB + KV_B
N_P = P_VB + KV_B
W_IF = 2 * QK_A + V_A
W_KB = W_IF + N_GATES
W_OA = W_KB + 2 * KV_B
N_IN = W_OA + 2 * V_A + 2 * Q_B + 2 * D_MODEL


def _sigmoid(x):
    return 1.0 / (1.0 + jnp.exp(-x))


def _silu(x):
    return x * _sigmoid(x)


def _log_sigmoid(x):
    return jnp.minimum(x, 0.0) - jnp.log(1.0 + jnp.exp(-jnp.abs(x)))


def _dot(a, b):
    return jnp.dot(a, b, preferred_element_type=F32)


def _dot_nt(a, b):
    return lax.dot_general(a, b, (((1,), (1,)), ((), ())), preferred_element_type=F32)


def _dot_tn(a, b):
    return lax.dot_general(a, b, (((0,), (0,)), ((), ())), preferred_element_type=F32)


def _params(semantics):
    return pltpu.CompilerParams(dimension_semantics=semantics, vmem_limit_bytes=VMEM_LIMIT)


def _pick(n, candidates):
    for c in candidates:
        if n % c == 0:
            return c
    raise ValueError(f"no tile in {candidates} divides {n}")


def _mod_kernel(c_ref, w_ref, b_ref, o_ref):
    a = _silu(c_ref[...]).astype(BF16)
    o_ref[...] = _dot(a, w_ref[...].astype(BF16)) + b_ref[...]


def _modulation(cc, w_mod, b_mod):
    rows = cc.shape[0]
    n = w_mod.shape[1]
    tn = _pick(n, (768, 512, 256, 128))
    return pl.pallas_call(
        _mod_kernel,
        out_shape=jax.ShapeDtypeStruct((rows, n), F32),
        grid=(n // tn,),
        in_specs=[pl.BlockSpec((rows, D_MODEL), lambda j: (0, 0)),
                  pl.BlockSpec((D_MODEL, tn), lambda j: (0, j)),
                  pl.BlockSpec((1, tn), lambda j: (0, j))],
        out_specs=pl.BlockSpec((rows, tn), lambda j: (0, j)),
        compiler_params=_params(("arbitrary",)),
        name="modulation",
    )(cc, w_mod, b_mod.reshape(1, n))


def _in_proj_kernel(x_ref, mod_ref, w_ref, wif_ref, bif_ref, p_ref, g_ref, u_ref, *, tm, bpb, ctx_row):
    i = pl.program_id(0)
    j = pl.program_id(1)

    @pl.when(j == 0)
    def _():
        for s in range(tm // TOK):
            t = i * (tm // TOK) + s
            row = jnp.where(t % bpb == 0, ctx_row, t // bpb)
            xs = x_ref[s * TOK:(s + 1) * TOK, :]
            mu = jnp.mean(xs, axis=-1, keepdims=True)
            xc = xs - mu
            var = jnp.mean(xc * xc, axis=-1, keepdims=True)
            y = xc * lax.rsqrt(var + EPS)
            shift = mod_ref[pl.ds(row, 1), 0:D_MODEL]
            scale = mod_ref[pl.ds(row, 1), D_MODEL:2 * D_MODEL]
            u_ref[s * TOK:(s + 1) * TOK, :] = (y * (1.0 + scale) + shift).astype(BF16)
        g_ref[...] = _dot(u_ref[...], wif_ref[...]) + bif_ref[...]

    p_ref[...] = _dot(u_ref[...], w_ref[...]).astype(BF16)


def _in_proj(xc, mod, w_p, w_if, b_if, bpb, ctx_row):
    nt = xc.shape[0]
    tm = _pick(nt, (1024, 512, 256))
    tn = 1024
    kern = functools.partial(_in_proj_kernel, tm=tm, bpb=bpb, ctx_row=ctx_row)
    return pl.pallas_call(
        kern,
        out_shape=(jax.ShapeDtypeStruct((nt, N_P), BF16), jax.ShapeDtypeStruct((nt, LANES), F32)),
        grid=(nt // tm, N_P // tn),
        in_specs=[pl.BlockSpec((tm, D_MODEL), lambda i, j: (i, 0)),
                  pl.BlockSpec(mod.shape, lambda i, j: (0, 0)),
                  pl.BlockSpec((D_MODEL, tn), lambda i, j: (0, j)),
                  pl.BlockSpec((D_MODEL, LANES), lambda i, j: (0, 0)),
                  pl.BlockSpec((1, LANES), lambda i, j: (0, 0))],
        out_specs=(pl.BlockSpec((tm, tn), lambda i, j: (i, j)),
                   pl.BlockSpec((tm, LANES), lambda i, j: (i, 0))),
        scratch_shapes=[pltpu.VMEM((tm, D_MODEL), BF16)],
        compiler_params=_params(("arbitrary", "arbitrary")),
        name="in_proj",
    )(xc, mod, w_p, w_if, b_if)


def _qk_conv_kernel(x_ref, prev_ref, next_ref, cw_ref, cb_ref, o_ref, *, bpb):
    tb = pl.program_id(0) % bpb
    has_prev = (tb >= 2).astype(F32)
    has_next = jnp.logical_and(tb >= 1, tb < bpb - 1).astype(F32)
    x = x_ref[...].astype(F32)
    prev_row = prev_ref[BF16_SUBLANES - 1:BF16_SUBLANES, :].astype(F32) * has_prev
    next_row = next_ref[0:1, :].astype(F32) * has_next
    rows = lax.broadcasted_iota(jnp.int32, (TOK, 1), 0)
    x_prev = jnp.where(rows == 0, prev_row, pltpu.roll(x, 1, axis=0))
    x_next = jnp.where(rows == TOK - 1, next_row, pltpu.roll(x, TOK - 1, axis=0))
    y = cb_ref[...] + x_prev * cw_ref[0:1, :] + x * cw_ref[1:2, :] + x_next * cw_ref[2:3, :]
    qk = _silu(y)
    o_ref[:, 0:QK_A] = qk[:, 0:QK_A].astype(BF16)
    o_ref[:, QK_A:2 * QK_A] = (qk[:, QK_A:2 * QK_A] * DK_A ** -0.5).astype(BF16)


def _qk_conv(p, conv_w, conv_b, bpb):
    nt = p.shape[0]
    nblk = nt // TOK
    sub = TOK // BF16_SUBLANES
    last = nt // BF16_SUBLANES - 1
    return pl.pallas_call(
        functools.partial(_qk_conv_kernel, bpb=bpb),
        out_shape=jax.ShapeDtypeStruct((nt, 2 * QK_A), BF16),
        grid=(nblk,),
        in_specs=[pl.BlockSpec((TOK, 2 * QK_A), lambda t: (t, P_QK // (2 * QK_A))),
                  pl.BlockSpec((BF16_SUBLANES, 2 * QK_A), lambda t: (jnp.maximum(t * sub - 1, 0), 0)),
                  pl.BlockSpec((BF16_SUBLANES, 2 * QK_A), lambda t: (jnp.minimum((t + 1) * sub, last), 0)),
                  pl.BlockSpec((3, 2 * QK_A), lambda t: (0, 0)),
                  pl.BlockSpec((1, 2 * QK_A), lambda t: (0, 0))],
        out_specs=pl.BlockSpec((TOK, 2 * QK_A), lambda t: (t, 0)),
        compiler_params=_params(("arbitrary",)),
        name="mlstm_qk_conv",
    )(p, p, p, conv_w, conv_b.reshape(1, -1))


def _rope(x, cos, sin_a, sin_b):
    half = ROT_HALF // 2
    return x * cos + pltpu.roll(x, HD_B - half, axis=1) * sin_a + pltpu.roll(x, half, axis=1) * sin_b


def _rms(x, w):
    return x * lax.rsqrt(jnp.mean(x * x, axis=-1, keepdims=True) + EPS) * w


def _kv_prep_kernel(k_ref, v_ref, cos_ref, sa_ref, sb_ref, kw_ref, ko_ref, vo_ref, *, bpb):
    is_lat = pl.program_id(0) % bpb > 0
    cos = jnp.where(is_lat, cos_ref[...], 1.0)
    sa = jnp.where(is_lat, sa_ref[...], 0.0)
    sb = jnp.where(is_lat, sb_ref[...], 0.0)
    for h in range(NKV_B):
        kh = k_ref[:, h * HD_B:(h + 1) * HD_B].astype(F32)
        ko_ref[0, h] = _rope(_rms(kh, kw_ref[...]), cos, sa, sb).astype(BF16)
        vo_ref[0, h] = v_ref[:, h * HD_B:(h + 1) * HD_B]


def _kv_prep(p, rope, k_norm_w, batch, bpb):
    nt = p.shape[0]
    tt = bpb * TOK
    tab = pl.BlockSpec((TOK, HD_B), lambda t: (jnp.maximum(t % bpb - 1, 0), 0))
    out = pl.BlockSpec((1, NKV_B, TOK, HD_B), lambda t: (t // bpb, 0, t % bpb, 0))
    return pl.pallas_call(
        functools.partial(_kv_prep_kernel, bpb=bpb),
        out_shape=(jax.ShapeDtypeStruct((batch, NKV_B, tt, HD_B), BF16),) * 2,
        grid=(nt // TOK,),
        in_specs=[pl.BlockSpec((TOK, KV_B), lambda t: (t, P_KB // KV_B)),
                  pl.BlockSpec((TOK, KV_B), lambda t: (t, P_VB // KV_B)),
                  tab, tab, tab,
                  pl.BlockSpec((1, HD_B), lambda t: (0, 0))],
        out_specs=(out, out),
        compiler_params=_params(("arbitrary",)),
        name="attn_kv_prep",
    )(p, p, *rope, k_norm_w.reshape(1, HD_B))


def _mlstm_kernel(*refs, reverse, fuse):
    if fuse:
        q_ref, k_ref, v_ref, g_ref, hb_ref, nw_ref, h_ref, c_ref, n_ref, m_ref = refs
    else:
        q_ref, k_ref, v_ref, g_ref, h_ref, c_ref, n_ref, m_ref = refs
    L = TOK

    @pl.when(pl.program_id(1) == 0)
    def _():
        c_ref[...] = jnp.zeros_like(c_ref)
        n_ref[...] = jnp.zeros_like(n_ref)
        m_ref[...] = jnp.full_like(m_ref, M_INIT)

    rows = lax.broadcasted_iota(jnp.int32, (L, L), 0)
    cols = lax.broadcasted_iota(jnp.int32, (L, L), 1)
    mask = (cols >= rows) if reverse else (cols <= rows)
    tri = mask.astype(BF16)

    g = g_ref[...]
    lf = _log_sigmoid(g)
    lf1 = lf.astype(BF16)
    r1 = lf - lf1.astype(F32)
    lf2 = r1.astype(BF16)
    lf3 = (r1 - lf2.astype(F32)).astype(BF16)
    bsum = _dot(tri, lf1) + _dot(tri, lf2) + _dot(tri, lf3)
    ib = g - pltpu.roll(bsum, LANES - NH_A, axis=1)
    ib_t = ib.T
    end = 0 if reverse else L - 1
    lane0 = 2 * NH_A if reverse else 0

    for h in range(NH_A):
        li = lane0 + h
        bcol = bsum[:, li + NH_A:li + NH_A + 1]
        ibcol = ib[:, li:li + 1]
        ibrow = ib_t[li:li + 1, :]
        m0 = m_ref[h:h + 1, 0:1]
        d = jnp.where(mask, bcol + ibrow, -jnp.inf)
        m_inter = bcol + m0
        m = jnp.maximum(m_inter, jnp.max(d, axis=1, keepdims=True))
        w = jnp.exp(d - m)
        a = jnp.exp(m_inter - m)
        qh = q_ref[:, h * DK_A:(h + 1) * DK_A]
        kh = k_ref[:, h * DK_A:(h + 1) * DK_A]
        vh = v_ref[:, h * DV_A:(h + 1) * DV_A]
        s = _dot_nt(qh, kh) * w
        c0 = c_ref[h]
        n0 = n_ref[h:h + 1, :]
        num = a * _dot_nt(qh, c0.astype(BF16)) + _dot(s.astype(BF16), vh)
        den = (a * jnp.sum(qh.astype(F32) * n0, axis=1, keepdims=True)
               + jnp.sum(s, axis=1, keepdims=True))
        hh = num / jnp.maximum(jnp.abs(den), jnp.exp(-m))
        if fuse:
            tot = hh + hb_ref[:, h * DV_A:(h + 1) * DV_A].astype(F32)
            hh = _rms(tot, nw_ref[:, h * DV_A:(h + 1) * DV_A])
        h_ref[:, h * DV_A:(h + 1) * DV_A] = hh.astype(h_ref.dtype)

        m_end = m[end:end + 1, :]
        a_end = a[end:end + 1, :]
        w_end = jnp.exp(bcol[end:end + 1, :] + ibcol - m_end)
        vw = (vh.astype(F32) * w_end).astype(BF16)
        c_ref[h] = a_end * c0 + _dot_tn(vw, kh)
        n_ref[h:h + 1, :] = a_end * n0 + jnp.sum(kh.astype(F32) * w_end, axis=0, keepdims=True)
        m_ref[h:h + 1, :] = jnp.broadcast_to(m_end, (1, LANES))


def _mlstm_sweep(qk, p, gates, batch, bpb, reverse, h_bwd=None, norm_w=None):
    nch = bpb - 1
    fuse = h_bwd is not None

    def src(b, c):
        if reverse:
            return b * bpb + jnp.where(c == 0, 0, bpb - c)
        return b * bpb + c

    def dst(b, c):
        if reverse:
            return b * nch + nch - jnp.maximum(c, 1)
        return b * nch + jnp.maximum(c, 1) - 1

    in_specs = [pl.BlockSpec((TOK, QK_A), lambda b, c: (src(b, c), 0)),
                pl.BlockSpec((TOK, QK_A), lambda b, c: (src(b, c), 1)),
                pl.BlockSpec((TOK, V_A), lambda b, c: (src(b, c), P_VA // V_A)),
                pl.BlockSpec((TOK, LANES), lambda b, c: (src(b, c), 0))]
    args = [qk, qk, p, gates]
    if fuse:
        in_specs += [pl.BlockSpec((TOK, V_A), lambda b, c: (dst(b, c), 0)),
                     pl.BlockSpec((1, V_A), lambda b, c: (0, 0))]
        args += [h_bwd, norm_w.reshape(1, V_A)]
    return pl.pallas_call(
        functools.partial(_mlstm_kernel, reverse=reverse, fuse=fuse),
        out_shape=jax.ShapeDtypeStruct((batch * nch * TOK, V_A), BF16),
        grid=(batch, bpb),
        in_specs=in_specs,
        out_specs=pl.BlockSpec((TOK, V_A), lambda b, c: (dst(b, c), 0)),
        scratch_shapes=[pltpu.VMEM((NH_A, DV_A, DK_A), F32),
                        pltpu.VMEM((NH_A, DK_A), F32),
                        pltpu.VMEM((NH_A, LANES), F32)],
        compiler_params=_params(("arbitrary", "arbitrary")),
        name="mlstm_bwd" if reverse else "mlstm_fwd",
    )(*args)


def _attn_kernel(q_ref, cos_ref, sa_ref, sb_ref, qw_ref, k_ref, v_ref, o_ref,
                 qs_ref, m_ref, l_ref, acc_ref, *, tk):
    tq = TOK
    cos, sa, sb = cos_ref[...], sa_ref[...], sb_ref[...]
    for h in range(GROUP_B):
        qh = q_ref[:, h * HD_B:(h + 1) * HD_B].astype(F32)
        qr = _rope(_rms(qh, qw_ref[...]), cos, sa, sb)
        qs_ref[h * tq:(h + 1) * tq, :] = (qr * HD_B ** -0.5).astype(BF16)
    m_ref[...] = jnp.full_like(m_ref, -jnp.inf)
    l_ref[...] = jnp.zeros_like(l_ref)
    acc_ref[...] = jnp.zeros_like(acc_ref)
    n_kv = k_ref.shape[2] // tk

    def body(c, carry):
        off = pl.multiple_of(c * tk, tk)
        kc = k_ref[0, 0, pl.ds(off, tk), :]
        vc = v_ref[0, 0, pl.ds(off, tk), :]
        for h in range(GROUP_B):
            sl = slice(h * tq, (h + 1) * tq)
            s = _dot_nt(qs_ref[sl, :], kc)
            m_old = m_ref[sl, :]
            m_new = jnp.maximum(m_old, jnp.max(s, axis=1, keepdims=True))
            pexp = jnp.exp(s - m_new)
            alpha = jnp.exp(m_old - m_new)
            l_ref[sl, :] = alpha * l_ref[sl, :] + jnp.sum(pexp, axis=1, keepdims=True)
            acc_ref[sl, :] = alpha * acc_ref[sl, :] + _dot(pexp.astype(BF16), vc)
            m_ref[sl, :] = m_new
        return carry

    lax.fori_loop(0, n_kv, body, 0)
    for h in range(GROUP_B):
        sl = slice(h * tq, (h + 1) * tq)
        o_ref[:, h * HD_B:(h + 1) * HD_B] = (acc_ref[sl, :] / l_ref[sl, :]).astype(BF16)


def _attention(p, k_all, v_all, rope, q_norm_w, batch, bpb):
    nch = bpb - 1
    tt = bpb * TOK
    tk = _pick(tt, (768, 512, 256))
    gw = GROUP_B * HD_B
    tab = pl.BlockSpec((TOK, HD_B), lambda b, g, i: (i, 0))
    kv = pl.BlockSpec((1, 1, tt, HD_B), lambda b, g, i: (b, g, 0, 0))
    return pl.pallas_call(
        functools.partial(_attn_kernel, tk=tk),
        out_shape=jax.ShapeDtypeStruct((batch * nch * TOK, Q_B), BF16),
        grid=(batch, NKV_B, nch),
        in_specs=[pl.BlockSpec((TOK, gw), lambda b, g, i: (b * bpb + 1 + i, P_QB // gw + g)),
                  tab, tab, tab,
                  pl.BlockSpec((1, HD_B), lambda b, g, i: (0, 0)),
                  kv, kv],
        out_specs=pl.BlockSpec((TOK, gw), lambda b, g, i: (b * nch + i, g)),
        scratch_shapes=[pltpu.VMEM((GROUP_B * TOK, HD_B), BF16),
                        pltpu.VMEM((GROUP_B * TOK, 1), F32),
                        pltpu.VMEM((GROUP_B * TOK, 1), F32),
                        pltpu.VMEM((GROUP_B * TOK, HD_B), F32)],
        compiler_params=_params(("arbitrary", "arbitrary", "arbitrary")),
        name="attention",
    )(p, *rope, q_norm_w.reshape(1, HD_B), k_all, v_all)


def _branch_kernel(hn_ref, oa_ref, za_ref, at_ref, zb_ref, ga_ref, gb_ref, wa_ref, wb_ref, y_ref):
    a = _sigmoid(oa_ref[...].astype(F32)) * hn_ref[...].astype(F32) * _silu(za_ref[...].astype(F32))
    ya = _dot(a.astype(BF16), wa_ref[...])
    b = at_ref[...].astype(F32) * _silu(zb_ref[...].astype(F32))
    yb = _dot(b.astype(BF16), wb_ref[...])
    y = _sigmoid(ga_ref[...].astype(F32)) * ya + _sigmoid(gb_ref[...].astype(F32)) * yb
    y_ref[...] = y.astype(BF16)


def _branches(hn, o_attn, p, w_ba, w_bb, bpb):
    nl = hn.shape[0]
    nch = bpb - 1
    tm = TOK

    def col(k):
        return pl.BlockSpec((tm, D_MODEL), lambda i: ((i // nch) * bpb + 1 + i % nch, k // D_MODEL))

    own = pl.BlockSpec((tm, D_MODEL), lambda i: (i, 0))
    resident = pl.BlockSpec(memory_space=pltpu.VMEM)
    return pl.pallas_call(
        _branch_kernel,
        out_shape=jax.ShapeDtypeStruct((nl, D_MODEL), BF16),
        grid=(nl // tm,),
        in_specs=[own, col(P_OA), col(P_ZA), own, col(P_ZB), col(P_GA), col(P_GB), resident, resident],
        out_specs=own,
        compiler_params=_params(("arbitrary",)),
        name="branch_merge",
    )(hn, p, p, o_attn, p, p, p, w_ba, w_bb)


def _out_kernel(y_ref, x_ref, mod_ref, w_ref, lw_ref, lb_ref, o_ref, *, tiles_per_sample):
    b = pl.program_id(0) // tiles_per_sample
    gate = mod_ref[pl.ds(b, 1), 2 * D_MODEL:3 * D_MODEL]
    z = ALPHA * x_ref[...] + gate * _dot(y_ref[...], w_ref[...])
    mu = jnp.mean(z, axis=-1, keepdims=True)
    zc = z - mu
    var = jnp.mean(zc * zc, axis=-1, keepdims=True)
    o_ref[...] = zc * lax.rsqrt(var + EPS) * lw_ref[...] + lb_ref[...]


def _out_proj(y, x2, mod, w_out, ln_w, ln_b, seq):
    nl = y.shape[0]
    tm = _pick(seq, (512, 256))
    tile = pl.BlockSpec((tm, D_MODEL), lambda i: (i, 0))
    vec = pl.BlockSpec((1, D_MODEL), lambda i: (0, 0))
    return pl.pallas_call(
        functools.partial(_out_kernel, tiles_per_sample=seq // tm),
        out_shape=jax.ShapeDtypeStruct((nl, D_MODEL), F32),
        grid=(nl // tm,),
        in_specs=[tile, tile, pl.BlockSpec(mod.shape, lambda i: (0, 0)),
                  pl.BlockSpec(memory_space=pltpu.VMEM), vec, vec],
        out_specs=tile,
        compiler_params=_params(("arbitrary",)),
        name="out_proj",
    )(y, x2, mod, w_out, ln_w.reshape(1, -1), ln_b.reshape(1, -1))


def _rope_tables(seq):
    t = jnp.arange(seq)
    inv = ROPE_THETA ** (-jnp.arange(0, ROT_HALF, 2, dtype=F32) / ROT_HALF)
    ang_r = (t // GRID_W).astype(F32)[:, None] * inv[None]
    ang_c = (t % GRID_W).astype(F32)[:, None] * inv[None]
    cr, sr, cc, sc = jnp.cos(ang_r), jnp.sin(ang_r), jnp.cos(ang_c), jnp.sin(ang_c)
    z = jnp.zeros_like(sr)
    cos = jnp.concatenate([cr, cr, cc, cc], axis=1)
    sin_a = jnp.concatenate([-sr, z, -sc, z], axis=1)
    sin_b = jnp.concatenate([z, sr, z, sc], axis=1)
    return cos, sin_a, sin_b


def _block_out(x, c, ctx, c_ctx, w_mod, b_mod, w_in, b_if, conv_w, conv_b, mh_norm_w,
               q_norm_w, k_norm_w, w_ba, w_bb):
    batch, seq, d = x.shape
    assert d == D_MODEL and ctx.shape == (batch, TOK, D_MODEL) and seq % TOK == 0 and seq % GRID_W == 0
    assert w_in.shape == (D_MODEL, N_IN)
    bpb = seq // TOK + 1

    rows = -(-(batch + 1) // 8) * 8
    cc = jnp.zeros((rows, D_MODEL), F32).at[:batch].set(c).at[batch].set(c_ctx)
    mod = _modulation(cc, w_mod, b_mod)

    xc = jnp.concatenate([ctx, x], axis=1).reshape(batch * bpb * TOK, D_MODEL)
    w_p = jnp.concatenate([w_in[:, :W_IF], w_in[:, W_OA:], w_in[:, W_KB:W_OA]], axis=1).astype(BF16)
    w_if = jnp.pad(w_in[:, W_IF:W_KB], ((0, 0), (0, LANES - N_GATES))).astype(BF16)
    b_if_p = jnp.pad(b_if, (0, LANES - N_GATES)).reshape(1, LANES)
    p, gates = _in_proj(xc, mod, w_p, w_if, b_if_p, bpb, batch)

    rope = _rope_tables(seq)
    qk = _qk_conv(p, conv_w, conv_b, bpb)
    k_all, v_all = _kv_prep(p, rope, k_norm_w, batch, bpb)
    h_bwd = _mlstm_sweep(qk, p, gates, batch, bpb, reverse=True)
    hn = _mlstm_sweep(qk, p, gates, batch, bpb, reverse=False, h_bwd=h_bwd, norm_w=mh_norm_w)
    o_attn = _attention(p, k_all, v_all, rope, q_norm_w, batch, bpb)
    y = _branches(hn, o_attn, p, w_ba.astype(BF16), w_bb.astype(BF16), bpb)
    return y, mod


def kernel(x, c, ctx, c_ctx, w_mod, b_mod, w_in, b_if, conv_w, conv_b, mh_norm_w, q_norm_w, k_norm_w,
           w_branch_a, w_branch_b, w_out, ln_w, ln_b):
    assert w_mod.shape[0] == DEPTH
    batch, seq, _ = x.shape
    y, mod = _block_out(x, c, ctx, c_ctx, w_mod[0], b_mod[0], w_in[0], b_if[0], conv_w[0], conv_b[0],
                        mh_norm_w[0], q_norm_w[0], k_norm_w[0], w_branch_a[0], w_branch_b[0])
    out = _out_proj(y, x.reshape(batch * seq, D_MODEL), mod, w_out[0].astype(BF16), ln_w[0], ln_b[0], seq)
    return out.reshape(batch, seq, D_MODEL)
```

```python
import functools

import jax
import jax.numpy as jnp
from jax import lax
from jax.experimental import pallas as pl
from jax.experimental.pallas import tpu as pltpu

F32 = jnp.float32
BF16 = jnp.bfloat16

D_MODEL = 2048
GRID_W = 64
NH_A, DK_A, DV_A = 8, 128, 256
QK_A, V_A = NH_A * DK_A, NH_A * DV_A
N_GATES = 4 * NH_A
NH_B, NKV_B, HD_B = 16, 4, 128
GROUP_B = NH_B // NKV_B
Q_B, KV_B = NH_B * HD_B, NKV_B * HD_B
ROT_HALF = HD_B // 2
ROPE_THETA = 10000.0
EPS = 1e-6
DEPTH = 1
ALPHA = (2 * DEPTH) ** 0.25
M_INIT = -1e30
LOG2_E = 1.4426950408889634

LANES = 128
BF16_SUBLANES = 16
VMEM_LIMIT = 56 * 1024 * 1024

TOK = 256

P_QK, P_VA, P_OA, P_ZA, P_QB, P_ZB, P_GA, P_GB = (i * D_MODEL for i in range(8))
P_KB = 8 * D_MODEL
P_VB = P_KB + KV_B
N_P = P_VB + KV_B
W_IF = 2 * QK_A + V_A
W_KB = W_IF + N_GATES
W_OA = W_KB + 2 * KV_B
N_IN = W_OA + 2 * V_A + 2 * Q_B + 2 * D_MODEL


def _sigmoid(x):
    return 1.0 / (1.0 + jnp.exp(-x))


def _silu(x):
    return x * _sigmoid(x)


def _log_sigmoid(x):
    return jnp.minimum(x, 0.0) - jnp.log(1.0 + jnp.exp(-jnp.abs(x)))


def _dot(a, b):
    return jnp.dot(a, b, preferred_element_type=F32)


def _dot_nt(a, b):
    return lax.dot_general(a, b, (((1,), (1,)), ((), ())), preferred_element_type=F32)


def _dot_tn(a, b):
    return lax.dot_general(a, b, (((0,), (0,)), ((), ())), preferred_element_type=F32)


def _params(semantics):
    return pltpu.CompilerParams(dimension_semantics=semantics, vmem_limit_bytes=VMEM_LIMIT)


def _pick(n, candidates):
    for c in candidates:
        if n % c == 0:
            return c
    raise ValueError(f"no tile in {candidates} divides {n}")


def _mod_kernel(c_ref, w_ref, b_ref, o_ref):
    a = _silu(c_ref[...]).astype(BF16)
    o_ref[...] = _dot(a, w_ref[...].astype(BF16)) + b_ref[...]


def _modulation(cc, w_mod, b_mod):
    rows = cc.shape[0]
    n = w_mod.shape[1]
    tn = _pick(n, (768, 512, 256, 128))
    return pl.pallas_call(
        _mod_kernel,
        out_shape=jax.ShapeDtypeStruct((rows, n), F32),
        grid=(n // tn,),
        in_specs=[pl.BlockSpec((rows, D_MODEL), lambda j: (0, 0)),
                  pl.BlockSpec((D_MODEL, tn), lambda j: (0, j)),
                  pl.BlockSpec((1, tn), lambda j: (0, j))],
        out_specs=pl.BlockSpec((rows, tn), lambda j: (0, j)),
        compiler_params=_params(("arbitrary",)),
        name="modulation",
    )(cc, w_mod, b_mod.reshape(1, n))


def _in_proj_kernel(x_ref, mod_ref, w_ref, wif_ref, bif_ref, p_ref, g_ref, u_ref, *, tm, bpb, ctx_row):
    i = pl.program_id(0)
    j = pl.program_id(1)

    @pl.when(j == 0)
    def _():
        for s in range(tm // TOK):
            t = i * (tm // TOK) + s
            row = jnp.where(t % bpb == 0, ctx_row, t // bpb)
            xs = x_ref[s * TOK:(s + 1) * TOK, :]
            mu = jnp.mean(xs, axis=-1, keepdims=True)
            xc = xs - mu
            var = jnp.mean(xc * xc, axis=-1, keepdims=True)
            y = xc * lax.rsqrt(var + EPS)
            shift = mod_ref[pl.ds(row, 1), 0:D_MODEL]
            scale = mod_ref[pl.ds(row, 1), D_MODEL:2 * D_MODEL]
            u_ref[s * TOK:(s + 1) * TOK, :] = (y * (1.0 + scale) + shift).astype(BF16)
        g_ref[...] = _dot(u_ref[...], wif_ref[...]) + bif_ref[...]

    p_ref[...] = _dot(u_ref[...], w_ref[...]).astype(BF16)


def _in_proj(xc, mod, w_p, w_if, b_if, bpb, ctx_row):
    nt = xc.shape[0]
    tm = _pick(nt, (1024, 512, 256))
    tn = 1024
    kern = functools.partial(_in_proj_kernel, tm=tm, bpb=bpb, ctx_row=ctx_row)
    return pl.pallas_call(
        kern,
        out_shape=(jax.ShapeDtypeStruct((nt, N_P), BF16), jax.ShapeDtypeStruct((nt, LANES), F32)),
        grid=(nt // tm, N_P // tn),
        in_specs=[pl.BlockSpec((tm, D_MODEL), lambda i, j: (i, 0)),
                  pl.BlockSpec(mod.shape, lambda i, j: (0, 0)),
                  pl.BlockSpec((D_MODEL, tn), lambda i, j: (0, j)),
                  pl.BlockSpec((D_MODEL, LANES), lambda i, j: (0, 0)),
                  pl.BlockSpec((1, LANES), lambda i, j: (0, 0))],
        out_specs=(pl.BlockSpec((tm, tn), lambda i, j: (i, j)),
                   pl.BlockSpec((tm, LANES), lambda i, j: (i, 0))),
        scratch_shapes=[pltpu.VMEM((tm, D_MODEL), BF16)],
        compiler_params=_params(("arbitrary", "arbitrary")),
        name="in_proj",
    )(xc, mod, w_p, w_if, b_if)


def _qk_conv_kernel(x_ref, prev_ref, next_ref, cw_ref, cb_ref, o_ref, *, bpb):
    tb = pl.program_id(0) % bpb
    has_prev = (tb >= 2).astype(F32)
    has_next = jnp.logical_and(tb >= 1, tb < bpb - 1).astype(F32)
    x = x_ref[...].astype(F32)
    prev_row = prev_ref[BF16_SUBLANES - 1:BF16_SUBLANES, :].astype(F32) * has_prev
    next_row = next_ref[0:1, :].astype(F32) * has_next
    rows = lax.broadcasted_iota(jnp.int32, (TOK, 1), 0)
    x_prev = jnp.where(rows == 0, prev_row, pltpu.roll(x, 1, axis=0))
    x_next = jnp.where(rows == TOK - 1, next_row, pltpu.roll(x, TOK - 1, axis=0))
    y = cb_ref[...] + x_prev * cw_ref[0:1, :] + x * cw_ref[1:2, :] + x_next * cw_ref[2:3, :]
    qk = _silu(y)
    o_ref[:, 0:QK_A] = qk[:, 0:QK_A].astype(BF16)
    o_ref[:, QK_A:2 * QK_A] = (qk[:, QK_A:2 * QK_A] * DK_A ** -0.5).astype(BF16)


def _qk_conv(p, conv_w, conv_b, bpb):
    nt = p.shape[0]
    nblk = nt // TOK
    sub = TOK // BF16_SUBLANES
    last = nt // BF16_SUBLANES - 1
    return pl.pallas_call(
        functools.partial(_qk_conv_kernel, bpb=bpb),
        out_shape=jax.ShapeDtypeStruct((nt, 2 * QK_A), BF16),
        grid=(nblk,),
        in_specs=[pl.BlockSpec((TOK, 2 * QK_A), lambda t: (t, P_QK // (2 * QK_A))),
                  pl.BlockSpec((BF16_SUBLANES, 2 * QK_A), lambda t: (jnp.maximum(t * sub - 1, 0), 0)),
                  pl.BlockSpec((BF16_SUBLANES, 2 * QK_A), lambda t: (jnp.minimum((t + 1) * sub, last), 0)),
                  pl.BlockSpec((3, 2 * QK_A), lambda t: (0, 0)),
                  pl.BlockSpec((1, 2 * QK_A), lambda t: (0, 0))],
        out_specs=pl.BlockSpec((TOK, 2 * QK_A), lambda t: (t, 0)),
        compiler_params=_params(("arbitrary",)),
        name="mlstm_qk_conv",
    )(p, p, p, conv_w, conv_b.reshape(1, -1))


def _rope(x, cos, sin_a, sin_b):
    half = ROT_HALF // 2
    return x * cos + pltpu.roll(x, HD_B - half, axis=1) * sin_a + pltpu.roll(x, half, axis=1) * sin_b


def _rms(x, w):
    return x * lax.rsqrt(jnp.mean(x * x, axis=-1, keepdims=True) + EPS) * w


def _kv_prep_kernel(k_ref, v_ref, cos_ref, sa_ref, sb_ref, kw_ref, ko_ref, vo_ref, *, bpb):
    is_lat = pl.program_id(0) % bpb > 0
    cos = jnp.where(is_lat, cos_ref[...], 1.0)
    sa = jnp.where(is_lat, sa_ref[...], 0.0)
    sb = jnp.where(is_lat, sb_ref[...], 0.0)
    for h in range(NKV_B):
        kh = k_ref[:, h * HD_B:(h + 1) * HD_B].astype(F32)
        ko_ref[0, h] = _rope(_rms(kh, kw_ref[...]), cos, sa, sb).astype(BF16)
        vo_ref[0, h] = v_ref[:, h * HD_B:(h + 1) * HD_B].astype(F32).T.astype(BF16)


def _kv_prep(p, rope, k_norm_w, batch, bpb):
    nt = p.shape[0]
    tt = bpb * TOK
    tab = pl.BlockSpec((TOK, HD_B), lambda t: (jnp.maximum(t % bpb - 1, 0), 0))
    return pl.pallas_call(
        functools.partial(_kv_prep_kernel, bpb=bpb),
        out_shape=(jax.ShapeDtypeStruct((batch, NKV_B, tt, HD_B), BF16),
                   jax.ShapeDtypeStruct((batch, NKV_B, HD_B, tt), BF16)),
        grid=(nt // TOK,),
        in_specs=[pl.BlockSpec((TOK, KV_B), lambda t: (t, P_KB // KV_B)),
                  pl.BlockSpec((TOK, KV_B), lambda t: (t, P_VB // KV_B)),
                  tab, tab, tab,
                  pl.BlockSpec((1, HD_B), lambda t: (0, 0))],
        out_specs=(pl.BlockSpec((1, NKV_B, TOK, HD_B), lambda t: (t // bpb, 0, t % bpb, 0)),
                   pl.BlockSpec((1, NKV_B, HD_B, TOK), lambda t: (t // bpb, 0, 0, t % bpb))),
        compiler_params=_params(("arbitrary",)),
        name="attn_kv_prep",
    )(p, p, *rope, k_norm_w.reshape(1, HD_B))


def _mlstm_kernel(*refs, reverse, fuse):
    if fuse:
        q_ref, k_ref, v_ref, g_ref, hb_ref, nw_ref, h_ref, c_ref, n_ref, m_ref = refs
    else:
        q_ref, k_ref, v_ref, g_ref, h_ref, c_ref, n_ref, m_ref = refs
    L = TOK

    @pl.when(pl.program_id(1) == 0)
    def _():
        c_ref[...] = jnp.zeros_like(c_ref)
        n_ref[...] = jnp.zeros_like(n_ref)
        m_ref[...] = jnp.full_like(m_ref, M_INIT)

    rows = lax.broadcasted_iota(jnp.int32, (L, L), 0)
    cols = lax.broadcasted_iota(jnp.int32, (L, L), 1)
    mask = (cols >= rows) if reverse else (cols <= rows)
    tri = mask.astype(BF16)

    g = g_ref[...]
    lf = _log_sigmoid(g)
    lf1 = lf.astype(BF16)
    r1 = lf - lf1.astype(F32)
    lf2 = r1.astype(BF16)
    lf3 = (r1 - lf2.astype(F32)).astype(BF16)
    bsum = _dot(tri, lf1) + _dot(tri, lf2) + _dot(tri, lf3)
    ib = g - pltpu.roll(bsum, LANES - NH_A, axis=1)
    ib_t = ib.T
    end = 0 if reverse else L - 1
    lane0 = 2 * NH_A if reverse else 0

    for h in range(NH_A):
        li = lane0 + h
        bcol = bsum[:, li + NH_A:li + NH_A + 1]
        ibcol = ib[:, li:li + 1]
        ibrow = ib_t[li:li + 1, :]
        m0 = m_ref[h:h + 1, 0:1]
        d = jnp.where(mask, bcol + ibrow, -jnp.inf)
        m_inter = bcol + m0
        m = jnp.maximum(m_inter, jnp.max(d, axis=1, keepdims=True))
        w = jnp.exp(d - m)
        a = jnp.exp(m_inter - m)
        qh = q_ref[:, h * DK_A:(h + 1) * DK_A]
        kh = k_ref[:, h * DK_A:(h + 1) * DK_A]
        vh = v_ref[:, h * DV_A:(h + 1) * DV_A]
        s = _dot_nt(qh, kh) * w
        c0 = c_ref[h]
        n0 = n_ref[h:h + 1, :]
        num = a * _dot_nt(qh, c0.astype(BF16)) + _dot(s.astype(BF16), vh)
        den = (a * jnp.sum(qh.astype(F32) * n0, axis=1, keepdims=True)
               + jnp.sum(s, axis=1, keepdims=True))
        hh = num / jnp.maximum(jnp.abs(den), jnp.exp(-m))
        if fuse:
            tot = hh + hb_ref[:, h * DV_A:(h + 1) * DV_A].astype(F32)
            hh = _rms(tot, nw_ref[:, h * DV_A:(h + 1) * DV_A])
        h_ref[:, h * DV_A:(h + 1) * DV_A] = hh.astype(h_ref.dtype)

        m_end = m[end:end + 1, :]
        a_end = a[end:end + 1, :]
        w_end = jnp.exp(bcol[end:end + 1, :] + ibcol - m_end)
        vw = (vh.astype(F32) * w_end).astype(BF16)
        c_ref[h] = a_end * c0 + _dot_tn(vw, kh)
        n_ref[h:h + 1, :] = a_end * n0 + jnp.sum(kh.astype(F32) * w_end, axis=0, keepdims=True)
        m_ref[h:h + 1, :] = jnp.broadcast_to(m_end, (1, LANES))


def _mlstm_sweep(qk, p, gates, batch, bpb, reverse, h_bwd=None, norm_w=None):
    nch = bpb - 1
    fuse = h_bwd is not None

    def src(b, c):
        if reverse:
            return b * bpb + jnp.where(c == 0, 0, bpb - c)
        return b * bpb + c

    def dst(b, c):
        if reverse:
            return b * nch + nch - jnp.maximum(c, 1)
        return b * nch + jnp.maximum(c, 1) - 1

    in_specs = [pl.BlockSpec((TOK, QK_A), lambda b, c: (src(b, c), 0)),
                pl.BlockSpec((TOK, QK_A), lambda b, c: (src(b, c), 1)),
                pl.BlockSpec((TOK, V_A), lambda b, c: (src(b, c), P_VA // V_A)),
                pl.BlockSpec((TOK, LANES), lambda b, c: (src(b, c), 0))]
    args = [qk, qk, p, gates]
    if fuse:
        in_specs += [pl.BlockSpec((TOK, V_A), lambda b, c: (dst(b, c), 0)),
                     pl.BlockSpec((1, V_A), lambda b, c: (0, 0))]
        args += [h_bwd, norm_w.reshape(1, V_A)]
    return pl.pallas_call(
        functools.partial(_mlstm_kernel, reverse=reverse, fuse=fuse),
        out_shape=jax.ShapeDtypeStruct((batch * nch * TOK, V_A), BF16),
        grid=(batch, bpb),
        in_specs=in_specs,
        out_specs=pl.BlockSpec((TOK, V_A), lambda b, c: (dst(b, c), 0)),
        scratch_shapes=[pltpu.VMEM((NH_A, DV_A, DK_A), F32),
                        pltpu.VMEM((NH_A, DK_A), F32),
                        pltpu.VMEM((NH_A, LANES), F32)],
        compiler_params=_params(("arbitrary", "arbitrary")),
        name="mlstm_bwd" if reverse else "mlstm_fwd",
    )(*args)


def _attn_kernel(q_ref, cos_ref, sa_ref, sb_ref, qw_ref, k_ref, vt_ref, o_ref,
                 qs_ref, s_ref, m_ref, l_ref, acc_ref, *, tk):
    tq = TOK
    cos, sa, sb = cos_ref[...], sa_ref[...], sb_ref[...]
    for h in range(GROUP_B):
        qh = q_ref[:, h * HD_B:(h + 1) * HD_B].astype(F32)
        qr = _rope(_rms(qh, qw_ref[...]), cos, sa, sb)
        qs_ref[h * tq:(h + 1) * tq, :] = (qr * (HD_B ** -0.5 * LOG2_E)).astype(BF16)
    m_ref[...] = jnp.full_like(m_ref, -jnp.inf)
    l_ref[...] = jnp.zeros_like(l_ref)
    acc_ref[...] = jnp.zeros_like(acc_ref)
    n_kv = k_ref.shape[2] // tk

    def scores(c, slot):
        off = pl.multiple_of(c * tk, tk)
        s_ref[slot] = _dot_nt(k_ref[0, 0, pl.ds(off, tk), :], qs_ref[...])

    def consume(c, slot):
        off = pl.multiple_of(c * tk, tk)
        s = s_ref[slot]
        m_old = m_ref[...]
        m_new = jnp.maximum(m_old, jnp.max(s, axis=0, keepdims=True))
        pexp = jnp.exp2(s - m_new)
        alpha = jnp.exp2(m_old - m_new)
        l_ref[...] = alpha * l_ref[...] + jnp.sum(pexp, axis=0, keepdims=True)
        acc_ref[...] = alpha * acc_ref[...] + _dot(vt_ref[0, 0, :, pl.ds(off, tk)], pexp.astype(BF16))
        m_ref[...] = m_new

    scores(0, 0)

    def pair(i, carry):
        scores(2 * i + 1, 1)
        consume(2 * i, 0)
        scores(2 * i + 2, 0)
        consume(2 * i + 1, 1)
        return carry

    n_pairs = (n_kv - 1) // 2
    lax.fori_loop(0, n_pairs, pair, 0)
    if n_kv % 2 == 0:
        scores(n_kv - 1, 1)
        consume(n_kv - 2, 0)
        consume(n_kv - 1, 1)
    else:
        consume(n_kv - 1, 0)

    o_t = acc_ref[...] / l_ref[...]
    for h in range(GROUP_B):
        o_ref[:, h * HD_B:(h + 1) * HD_B] = o_t[:, h * tq:(h + 1) * tq].T.astype(BF16)


def _attention(p, k_all, v_all, rope, q_norm_w, batch, bpb):
    nch = bpb - 1
    tt = bpb * TOK
    tk = _pick(tt, (768, 512, 256))
    gw = GROUP_B * HD_B
    nq = GROUP_B * TOK
    tab = pl.BlockSpec((TOK, HD_B), lambda b, g, i: (i, 0))
    return pl.pallas_call(
        functools.partial(_attn_kernel, tk=tk),
        out_shape=jax.ShapeDtypeStruct((batch * nch * TOK, Q_B), BF16),
        grid=(batch, NKV_B, nch),
        in_specs=[pl.BlockSpec((TOK, gw), lambda b, g, i: (b * bpb + 1 + i, P_QB // gw + g)),
                  tab, tab, tab,
                  pl.BlockSpec((1, HD_B), lambda b, g, i: (0, 0)),
                  pl.BlockSpec((1, 1, tt, HD_B), lambda b, g, i: (b, g, 0, 0)),
                  pl.BlockSpec((1, 1, HD_B, tt), lambda b, g, i: (b, g, 0, 0))],
        out_specs=pl.BlockSpec((TOK, gw), lambda b, g, i: (b * nch + i, g)),
        scratch_shapes=[pltpu.VMEM((nq, HD_B), BF16),
                        pltpu.VMEM((2, tk, nq), F32),
                        pltpu.VMEM((1, nq), F32),
                        pltpu.VMEM((1, nq), F32),
                        pltpu.VMEM((HD_B, nq), F32)],
        compiler_params=_params(("arbitrary", "arbitrary", "arbitrary")),
        name="attention",
    )(p, *rope, q_norm_w.reshape(1, HD_B), k_all, v_all)


def _branch_kernel(hn_ref, oa_ref, za_ref, at_ref, zb_ref, ga_ref, gb_ref, wa_ref, wb_ref, y_ref):
    a = _sigmoid(oa_ref[...].astype(F32)) * hn_ref[...].astype(F32) * _silu(za_ref[...].astype(F32))
    ya = _dot(a.astype(BF16), wa_ref[...])
    b = at_ref[...].astype(F32) * _silu(zb_ref[...].astype(F32))
    yb = _dot(b.astype(BF16), wb_ref[...])
    y = _sigmoid(ga_ref[...].astype(F32)) * ya + _sigmoid(gb_ref[...].astype(F32)) * yb
    y_ref[...] = y.astype(BF16)


def _branches(hn, o_attn, p, w_ba, w_bb, bpb):
    nl = hn.shape[0]
    nch = bpb - 1
    tm = TOK

    def col(k):
        return pl.BlockSpec((tm, D_MODEL), lambda i: ((i // nch) * bpb + 1 + i % nch, k // D_MODEL))

    own = pl.BlockSpec((tm, D_MODEL), lambda i: (i, 0))
    resident = pl.BlockSpec(memory_space=pltpu.VMEM)
    return pl.pallas_call(
        _branch_kernel,
        out_shape=jax.ShapeDtypeStruct((nl, D_MODEL), BF16),
        grid=(nl // tm,),
        in_specs=[own, col(P_OA), col(P_ZA), own, col(P_ZB), col(P_GA), col(P_GB), resident, resident],
        out_specs=own,
        compiler_params=_params(("arbitrary",)),
        name="branch_merge",
    )(hn, p, p, o_attn, p, p, p, w_ba, w_bb)


def _out_kernel(y_ref, x_ref, mod_ref, w_ref, lw_ref, lb_ref, o_ref, *, tiles_per_sample):
    b = pl.program_id(0) // tiles_per_sample
    gate = mod_ref[pl.ds(b, 1), 2 * D_MODEL:3 * D_MODEL]
    z = ALPHA * x_ref[...] + gate * _dot(y_ref[...], w_ref[...])
    mu = jnp.mean(z, axis=-1, keepdims=True)
    zc = z - mu
    var = jnp.mean(zc * zc, axis=-1, keepdims=True)
    o_ref[...] = zc * lax.rsqrt(var + EPS) * lw_ref[...] + lb_ref[...]


def _out_proj(y, x2, mod, w_out, ln_w, ln_b, seq):
    nl = y.shape[0]
    tm = _pick(seq, (512, 256))
    tile = pl.BlockSpec((tm, D_MODEL), lambda i: (i, 0))
    vec = pl.BlockSpec((1, D_MODEL), lambda i: (0, 0))
    return pl.pallas_call(
        functools.partial(_out_kernel, tiles_per_sample=seq // tm),
        out_shape=jax.ShapeDtypeStruct((nl, D_MODEL), F32),
        grid=(nl // tm,),
        in_specs=[tile, tile, pl.BlockSpec(mod.shape, lambda i: (0, 0)),
                  pl.BlockSpec(memory_space=pltpu.VMEM), vec, vec],
        out_specs=tile,
        compiler_params=_params(("arbitrary",)),
        name="out_proj",
    )(y, x2, mod, w_out, ln_w.reshape(1, -1), ln_b.reshape(1, -1))


def _rope_tables(seq):
    t = jnp.arange(seq)
    inv = ROPE_THETA ** (-jnp.arange(0, ROT_HALF, 2, dtype=F32) / ROT_HALF)
    ang_r = (t // GRID_W).astype(F32)[:, None] * inv[None]
    ang_c = (t % GRID_W).astype(F32)[:, None] * inv[None]
    cr, sr, cc, sc = jnp.cos(ang_r), jnp.sin(ang_r), jnp.cos(ang_c), jnp.sin(ang_c)
    z = jnp.zeros_like(sr)
    cos = jnp.concatenate([cr, cr, cc, cc], axis=1)
    sin_a = jnp.concatenate([-sr, z, -sc, z], axis=1)
    sin_b = jnp.concatenate([z, sr, z, sc], axis=1)
    return cos, sin_a, sin_b


def _block_out(x, c, ctx, c_ctx, w_mod, b_mod, w_in, b_if, conv_w, conv_b, mh_norm_w,
               q_norm_w, k_norm_w, w_ba, w_bb):
    batch, seq, d = x.shape
    assert d == D_MODEL and ctx.shape == (batch, TOK, D_MODEL) and seq % TOK == 0 and seq % GRID_W == 0
    assert w_in.shape == (D_MODEL, N_IN)
    bpb = seq // TOK + 1

    rows = -(-(batch + 1) // 8) * 8
    cc = jnp.zeros((rows, D_MODEL), F32).at[:batch].set(c).at[batch].set(c_ctx)
    mod = _modulation(cc, w_mod, b_mod)

    xc = jnp.concatenate([ctx, x], axis=1).reshape(batch * bpb * TOK, D_MODEL)
    w_p = jnp.concatenate([w_in[:, :W_IF], w_in[:, W_OA:], w_in[:, W_KB:W_OA]], axis=1).astype(BF16)
    w_if = jnp.pad(w_in[:, W_IF:W_KB], ((0, 0), (0, LANES - N_GATES))).astype(BF16)
    b_if_p = jnp.pad(b_if, (0, LANES - N_GATES)).reshape(1, LANES)
    p, gates = _in_proj(xc, mod, w_p, w_if, b_if_p, bpb, batch)

    rope = _rope_tables(seq)
    qk = _qk_conv(p, conv_w, conv_b, bpb)
    k_all, v_all = _kv_prep(p, rope, k_norm_w, batch, bpb)
    h_bwd = _mlstm_sweep(qk, p, gates, batch, bpb, reverse=True)
    hn = _mlstm_sweep(qk, p, gates, batch, bpb, reverse=False, h_bwd=h_bwd, norm_w=mh_norm_w)
    o_attn = _attention(p, k_all, v_all, rope, q_norm_w, batch, bpb)
    y = _branches(hn, o_attn, p, w_ba.astype(BF16), w_bb.astype(BF16), bpb)
    return y, mod


def kernel(x, c, ctx, c_ctx, w_mod, b_mod, w_in, b_if, conv_w, conv_b, mh_norm_w, q_norm_w, k_norm_w,
           w_branch_a, w_branch_b, w_out, ln_w, ln_b):
    assert w_mod.shape[0] == DEPTH
    batch, seq, _ = x.shape
    y, mod = _block_out(x, c, ctx, c_ctx, w_mod[0], b_mod[0], w_in[0], b_if[0], conv_w[0], conv_b[0],
                        mh_norm_w[0], q_norm_w[0], k_norm_w[0], w_branch_a[0], w_branch_b[0])
    out = _out_proj(y, x.reshape(batch * seq, D_MODEL), mod, w_out[0].astype(BF16), ln_w[0], ln_b[0], seq)
    return out.reshape(batch, seq, D_MODEL)
```

```python
import functools

import jax
import jax.numpy as jnp
from jax import lax
from jax.experimental import pallas as pl
from jax.experimental.pallas import tpu as pltpu

F32 = jnp.float32
BF16 = jnp.bfloat16

D_MODEL = 2048
GRID_W = 64
NH_A, DK_A, DV_A = 8, 128, 256
QK_A, V_A = NH_A * DK_A, NH_A * DV_A
N_GATES = 4 * NH_A
NH_B, NKV_B, HD_B = 16, 4, 128
GROUP_B = NH_B // NKV_B
Q_B, KV_B = NH_B * HD_B, NKV_B * HD_B
ROT_HALF = HD_B // 2
ROPE_THETA = 10000.0
EPS = 1e-6
DEPTH = 1
ALPHA = (2 * DEPTH) ** 0.25
M_INIT = -1e30
LOG2_E = 1.4426950408889634

LANES = 128
BF16_SUBLANES = 16
VMEM_LIMIT = 56 * 1024 * 1024

TOK = 256

P_QK, P_VA, P_OA, P_ZA, P_QB, P_ZB, P_GA, P_GB = (i * D_MODEL for i in range(8))
P_KB = 8 * D_MODEL
P_VB = P_KB + KV_B
N_P = P_VB + KV_B
W_IF = 2 * QK_A + V_A
W_KB = W_IF + N_GATES
W_OA = W_KB + 2 * KV_B
N_IN = W_OA + 2 * V_A + 2 * Q_B + 2 * D_MODEL


def _sigmoid(x):
    return 1.0 / (1.0 + jnp.exp(-x))


def _silu(x):
    return x * _sigmoid(x)


def _log_sigmoid(x):
    return jnp.minimum(x, 0.0) - jnp.log(1.0 + jnp.exp(-jnp.abs(x)))


def _dot(a, b):
    return jnp.dot(a, b, preferred_element_type=F32)


def _dot_nt(a, b):
    return lax.dot_general(a, b, (((1,), (1,)), ((), ())), preferred_element_type=F32)


def _dot_tn(a, b):
    return lax.dot_general(a, b, (((0,), (0,)), ((), ())), preferred_element_type=F32)


def _params(semantics):
    return pltpu.CompilerParams(dimension_semantics=semantics, vmem_limit_bytes=VMEM_LIMIT)


def _pick(n, candidates):
    for c in candidates:
        if n % c == 0:
            return c
    raise ValueError(f"no tile in {candidates} divides {n}")


def _mod_kernel(c_ref, w_ref, b_ref, o_ref):
    a = _silu(c_ref[...]).astype(BF16)
    o_ref[...] = _dot(a, w_ref[...].astype(BF16)) + b_ref[...]


def _modulation(cc, w_mod, b_mod):
    rows = cc.shape[0]
    n = w_mod.shape[1]
    tn = _pick(n, (768, 512, 256, 128))
    return pl.pallas_call(
        _mod_kernel,
        out_shape=jax.ShapeDtypeStruct((rows, n), F32),
        grid=(n // tn,),
        in_specs=[pl.BlockSpec((rows, D_MODEL), lambda j: (0, 0)),
                  pl.BlockSpec((D_MODEL, tn), lambda j: (0, j)),
                  pl.BlockSpec((1, tn), lambda j: (0, j))],
        out_specs=pl.BlockSpec((rows, tn), lambda j: (0, j)),
        compiler_params=_params(("arbitrary",)),
        name="modulation",
    )(cc, w_mod, b_mod.reshape(1, n))


def _in_proj_kernel(x_ref, mod_ref, w_ref, wif_ref, bif_ref, p_ref, g_ref, u_ref, *, tm, bpb, ctx_row):
    i = pl.program_id(0)
    j = pl.program_id(1)

    @pl.when(j == 0)
    def _():
        for s in range(tm // TOK):
            t = i * (tm // TOK) + s
            row = jnp.where(t % bpb == 0, ctx_row, t // bpb)
            xs = x_ref[s * TOK:(s + 1) * TOK, :]
            mu = jnp.mean(xs, axis=-1, keepdims=True)
            xc = xs - mu
            var = jnp.mean(xc * xc, axis=-1, keepdims=True)
            y = xc * lax.rsqrt(var + EPS)
            shift = mod_ref[pl.ds(row, 1), 0:D_MODEL]
            scale = mod_ref[pl.ds(row, 1), D_MODEL:2 * D_MODEL]
            u_ref[s * TOK:(s + 1) * TOK, :] = (y * (1.0 + scale) + shift).astype(BF16)
        g_ref[...] = _dot(u_ref[...], wif_ref[...]) + bif_ref[...]

    p_ref[...] = _dot(u_ref[...], w_ref[...]).astype(BF16)


def _in_proj(xc, mod, w_p, w_if, b_if, bpb, ctx_row):
    nt = xc.shape[0]
    tm = _pick(nt, (1024, 512, 256))
    tn = 1024
    kern = functools.partial(_in_proj_kernel, tm=tm, bpb=bpb, ctx_row=ctx_row)
    return pl.pallas_call(
        kern,
        out_shape=(jax.ShapeDtypeStruct((nt, N_P), BF16), jax.ShapeDtypeStruct((nt, LANES), F32)),
        grid=(nt // tm, N_P // tn),
        in_specs=[pl.BlockSpec((tm, D_MODEL), lambda i, j: (i, 0)),
                  pl.BlockSpec(mod.shape, lambda i, j: (0, 0)),
                  pl.BlockSpec((D_MODEL, tn), lambda i, j: (0, j)),
                  pl.BlockSpec((D_MODEL, LANES), lambda i, j: (0, 0)),
                  pl.BlockSpec((1, LANES), lambda i, j: (0, 0))],
        out_specs=(pl.BlockSpec((tm, tn), lambda i, j: (i, j)),
                   pl.BlockSpec((tm, LANES), lambda i, j: (i, 0))),
        scratch_shapes=[pltpu.VMEM((tm, D_MODEL), BF16)],
        compiler_params=_params(("arbitrary", "arbitrary")),
        name="in_proj",
    )(xc, mod, w_p, w_if, b_if)


def _qk_conv_kernel(x_ref, prev_ref, next_ref, cw_ref, cb_ref, o_ref, *, bpb):
    tb = pl.program_id(0) % bpb
    has_prev = (tb >= 2).astype(F32)
    has_next = jnp.logical_and(tb >= 1, tb < bpb - 1).astype(F32)
    x = x_ref[...].astype(F32)
    prev_row = prev_ref[BF16_SUBLANES - 1:BF16_SUBLANES, :].astype(F32) * has_prev
    next_row = next_ref[0:1, :].astype(F32) * has_next
    rows = lax.broadcasted_iota(jnp.int32, (TOK, 1), 0)
    x_prev = jnp.where(rows == 0, prev_row, pltpu.roll(x, 1, axis=0))
    x_next = jnp.where(rows == TOK - 1, next_row, pltpu.roll(x, TOK - 1, axis=0))
    y = cb_ref[...] + x_prev * cw_ref[0:1, :] + x * cw_ref[1:2, :] + x_next * cw_ref[2:3, :]
    qk = _silu(y)
    o_ref[:, 0:QK_A] = qk[:, 0:QK_A].astype(BF16)
    o_ref[:, QK_A:2 * QK_A] = (qk[:, QK_A:2 * QK_A] * DK_A ** -0.5).astype(BF16)


def _qk_conv(p, conv_w, conv_b, bpb):
    nt = p.shape[0]
    nblk = nt // TOK
    sub = TOK // BF16_SUBLANES
    last = nt // BF16_SUBLANES - 1
    return pl.pallas_call(
        functools.partial(_qk_conv_kernel, bpb=bpb),
        out_shape=jax.ShapeDtypeStruct((nt, 2 * QK_A), BF16),
        grid=(nblk,),
        in_specs=[pl.BlockSpec((TOK, 2 * QK_A), lambda t: (t, P_QK // (2 * QK_A))),
                  pl.BlockSpec((BF16_SUBLANES, 2 * QK_A), lambda t: (jnp.maximum(t * sub - 1, 0), 0)),
                  pl.BlockSpec((BF16_SUBLANES, 2 * QK_A), lambda t: (jnp.minimum((t + 1) * sub, last), 0)),
                  pl.BlockSpec((3, 2 * QK_A), lambda t: (0, 0)),
                  pl.BlockSpec((1, 2 * QK_A), lambda t: (0, 0))],
        out_specs=pl.BlockSpec((TOK, 2 * QK_A), lambda t: (t, 0)),
        compiler_params=_params(("arbitrary",)),
        name="mlstm_qk_conv",
    )(p, p, p, conv_w, conv_b.reshape(1, -1))


def _rope(x, cos, sin_a, sin_b):
    half = ROT_HALF // 2
    return x * cos + pltpu.roll(x, HD_B - half, axis=1) * sin_a + pltpu.roll(x, half, axis=1) * sin_b


def _rms(x, w):
    return x * lax.rsqrt(jnp.mean(x * x, axis=-1, keepdims=True) + EPS) * w


def _kv_prep_kernel(k_ref, v_ref, cos_ref, sa_ref, sb_ref, kw_ref, ko_ref, vo_ref, *, bpb):
    is_lat = pl.program_id(0) % bpb > 0
    cos = jnp.where(is_lat, cos_ref[...], 1.0)
    sa = jnp.where(is_lat, sa_ref[...], 0.0)
    sb = jnp.where(is_lat, sb_ref[...], 0.0)
    for h in range(NKV_B):
        kh = k_ref[:, h * HD_B:(h + 1) * HD_B].astype(F32)
        ko_ref[0, h] = _rope(_rms(kh, kw_ref[...]), cos, sa, sb).astype(BF16)
        vo_ref[0, h] = v_ref[:, h * HD_B:(h + 1) * HD_B].astype(F32).T.astype(BF16)


def _kv_prep(p, rope, k_norm_w, batch, bpb):
    nt = p.shape[0]
    tt = bpb * TOK
    tab = pl.BlockSpec((TOK, HD_B), lambda t: (jnp.maximum(t % bpb - 1, 0), 0))
    return pl.pallas_call(
        functools.partial(_kv_prep_kernel, bpb=bpb),
        out_shape=(jax.ShapeDtypeStruct((batch, NKV_B, tt, HD_B), BF16),
                   jax.ShapeDtypeStruct((batch, NKV_B, HD_B, tt), BF16)),
        grid=(nt // TOK,),
        in_specs=[pl.BlockSpec((TOK, KV_B), lambda t: (t, P_KB // KV_B)),
                  pl.BlockSpec((TOK, KV_B), lambda t: (t, P_VB // KV_B)),
                  tab, tab, tab,
                  pl.BlockSpec((1, HD_B), lambda t: (0, 0))],
        out_specs=(pl.BlockSpec((1, NKV_B, TOK, HD_B), lambda t: (t // bpb, 0, t % bpb, 0)),
                   pl.BlockSpec((1, NKV_B, HD_B, TOK), lambda t: (t // bpb, 0, 0, t % bpb))),
        compiler_params=_params(("arbitrary",)),
        name="attn_kv_prep",
    )(p, p, *rope, k_norm_w.reshape(1, HD_B))


def _mlstm_kernel(*refs, reverse, fuse):
    if fuse:
        q_ref, k_ref, v_ref, g_ref, hb_ref, nw_ref, h_ref, c_ref, n_ref, m_ref = refs
    else:
        q_ref, k_ref, v_ref, g_ref, h_ref, c_ref, n_ref, m_ref = refs
    L = TOK

    @pl.when(pl.program_id(1) == 0)
    def _():
        c_ref[...] = jnp.zeros_like(c_ref)
        n_ref[...] = jnp.zeros_like(n_ref)
        m_ref[...] = jnp.full_like(m_ref, M_INIT)

    rows = lax.broadcasted_iota(jnp.int32, (L, L), 0)
    cols = lax.broadcasted_iota(jnp.int32, (L, L), 1)
    mask = (cols >= rows) if reverse else (cols <= rows)
    tri = mask.astype(BF16)

    g = g_ref[...]
    lf = _log_sigmoid(g)
    lf1 = lf.astype(BF16)
    r1 = lf - lf1.astype(F32)
    lf2 = r1.astype(BF16)
    lf3 = (r1 - lf2.astype(F32)).astype(BF16)
    bsum = _dot(tri, lf1) + _dot(tri, lf2) + _dot(tri, lf3)
    ib = g - pltpu.roll(bsum, LANES - NH_A, axis=1)
    ib_t = ib.T
    end = 0 if reverse else L - 1
    lane0 = 2 * NH_A if reverse else 0

    for h in range(NH_A):
        li = lane0 + h
        bcol = bsum[:, li + NH_A:li + NH_A + 1]
        ibcol = ib[:, li:li + 1]
        ibrow = ib_t[li:li + 1, :]
        m0 = m_ref[h:h + 1, 0:1]
        d = jnp.where(mask, bcol + ibrow, -jnp.inf)
        m_inter = bcol + m0
        m = jnp.maximum(m_inter, jnp.max(d, axis=1, keepdims=True))
        w = jnp.exp(d - m)
        a = jnp.exp(m_inter - m)
        qh = q_ref[:, h * DK_A:(h + 1) * DK_A]
        kh = k_ref[:, h * DK_A:(h + 1) * DK_A]
        vh = v_ref[:, h * DV_A:(h + 1) * DV_A]
        s = _dot_nt(qh, kh) * w
        c0 = c_ref[h]
        n0 = n_ref[h:h + 1, :]
        num = a * _dot_nt(qh, c0.astype(BF16)) + _dot(s.astype(BF16), vh)
        den = (a * jnp.sum(qh.astype(F32) * n0, axis=1, keepdims=True)
               + jnp.sum(s, axis=1, keepdims=True))
        hh = num / jnp.maximum(jnp.abs(den), jnp.exp(-m))
        if fuse:
            tot = hh + hb_ref[:, h * DV_A:(h + 1) * DV_A].astype(F32)
            hh = _rms(tot, nw_ref[:, h * DV_A:(h + 1) * DV_A])
        h_ref[:, h * DV_A:(h + 1) * DV_A] = hh.astype(h_ref.dtype)

        m_end = m[end:end + 1, :]
        a_end = a[end:end + 1, :]
        w_end = jnp.exp(bcol[end:end + 1, :] + ibcol - m_end)
        vw = (vh.astype(F32) * w_end).astype(BF16)
        c_ref[h] = a_end * c0 + _dot_tn(vw, kh)
        n_ref[h:h + 1, :] = a_end * n0 + jnp.sum(kh.astype(F32) * w_end, axis=0, keepdims=True)
        m_ref[h:h + 1, :] = jnp.broadcast_to(m_end, (1, LANES))


def _mlstm_sweep(qk, p, gates, batch, bpb, reverse, h_bwd=None, norm_w=None):
    nch = bpb - 1
    fuse = h_bwd is not None

    def src(b, c):
        if reverse:
            return b * bpb + jnp.where(c == 0, 0, bpb - c)
        return b * bpb + c

    def dst(b, c):
        if reverse:
            return b * nch + nch - jnp.maximum(c, 1)
        return b * nch + jnp.maximum(c, 1) - 1

    in_specs = [pl.BlockSpec((TOK, QK_A), lambda b, c: (src(b, c), 0)),
                pl.BlockSpec((TOK, QK_A), lambda b, c: (src(b, c), 1)),
                pl.BlockSpec((TOK, V_A), lambda b, c: (src(b, c), P_VA // V_A)),
                pl.BlockSpec((TOK, LANES), lambda b, c: (src(b, c), 0))]
    args = [qk, qk, p, gates]
    if fuse:
        in_specs += [pl.BlockSpec((TOK, V_A), lambda b, c: (dst(b, c), 0)),
                     pl.BlockSpec((1, V_A), lambda b, c: (0, 0))]
        args += [h_bwd, norm_w.reshape(1, V_A)]
    return pl.pallas_call(
        functools.partial(_mlstm_kernel, reverse=reverse, fuse=fuse),
        out_shape=jax.ShapeDtypeStruct((batch * nch * TOK, V_A), BF16),
        grid=(batch, bpb),
        in_specs=in_specs,
        out_specs=pl.BlockSpec((TOK, V_A), lambda b, c: (dst(b, c), 0)),
        scratch_shapes=[pltpu.VMEM((NH_A, DV_A, DK_A), F32),
                        pltpu.VMEM((NH_A, DK_A), F32),
                        pltpu.VMEM((NH_A, LANES), F32)],
        compiler_params=_params(("arbitrary", "arbitrary")),
        name="mlstm_bwd" if reverse else "mlstm_fwd",
    )(*args)


ATTN_UNROLL = 4


def _attn_kernel(q_ref, cos_ref, sa_ref, sb_ref, qn_ref, cosn_ref, san_ref, sbn_ref, qw_ref,
                 k_ref, kn_ref, vt_ref, o_ref, qs_ref, s_ref, m_ref, l_ref, acc_ref, *, tk):
    tq = TOK
    n_kv = k_ref.shape[2] // tk
    assert n_kv % 2 == 0

    def prep(qr_ref, c_ref, a_ref, b_ref):
        cos, sa, sb = c_ref[...], a_ref[...], b_ref[...]
        for h in range(GROUP_B):
            qh = qr_ref[:, h * HD_B:(h + 1) * HD_B].astype(F32)
            qr = _rope(_rms(qh, qw_ref[...]), cos, sa, sb)
            qs_ref[h * tq:(h + 1) * tq, :] = (qr * (HD_B ** -0.5 * LOG2_E)).astype(BF16)

    def scores(c, slot):
        off = pl.multiple_of(c * tk, tk)
        s_ref[slot] = _dot_nt(k_ref[0, 0, pl.ds(off, tk), :], qs_ref[...])

    def consume(c, slot):
        off = pl.multiple_of(c * tk, LANES)
        s = s_ref[slot]
        m_old = m_ref[...]
        m_new = jnp.maximum(m_old, jnp.max(s, axis=0, keepdims=True))
        pexp = jnp.exp2(s - m_new)
        alpha = jnp.exp2(m_old - m_new)
        l_ref[...] = alpha * l_ref[...] + jnp.sum(pexp, axis=0, keepdims=True)
        acc_ref[...] = alpha * acc_ref[...] + _dot(vt_ref[0, 0, :, pl.ds(off, tk)], pexp.astype(BF16))
        m_ref[...] = m_new

    first = jnp.logical_and(pl.program_id(0) == 0,
                            jnp.logical_and(pl.program_id(1) == 0, pl.program_id(2) == 0))

    @pl.when(first)
    def _():
        prep(q_ref, cos_ref, sa_ref, sb_ref)
        scores(0, 0)

    m_ref[...] = jnp.full_like(m_ref, -jnp.inf)
    l_ref[...] = jnp.zeros_like(l_ref)
    acc_ref[...] = jnp.zeros_like(acc_ref)

    def trip(t, carry):
        for u in range(ATTN_UNROLL):
            c = ATTN_UNROLL * t + u
            scores(c + 1, (u + 1) % 2)
            consume(c, u % 2)
        return carry

    n_trips = (n_kv - 2) // ATTN_UNROLL
    lax.fori_loop(0, n_trips, trip, 0)
    for c in range(ATTN_UNROLL * n_trips, n_kv - 1):
        scores(c + 1, (c + 1) % 2)
        consume(c, c % 2)
    prep(qn_ref, cosn_ref, san_ref, sbn_ref)
    s_ref[0] = _dot_nt(kn_ref[0, 0], qs_ref[...])
    consume(n_kv - 1, 1)

    o_t = acc_ref[...] / l_ref[...]
    for h in range(GROUP_B):
        o_ref[:, h * HD_B:(h + 1) * HD_B] = o_t[:, h * tq:(h + 1) * tq].T.astype(BF16)


def _attention(p, k_all, v_all, rope, q_norm_w, batch, bpb):
    nch = bpb - 1
    tt = bpb * TOK
    tk = next(c for c in (384, 256, 128) if tt % c == 0 and (tt // c) % 2 == 0)
    gw = GROUP_B * HD_B
    nq = GROUP_B * TOK

    def nxt(b, g, i):
        i2 = (i + 1) % nch
        g1 = g + (i + 1) // nch
        return jnp.minimum(b + g1 // NKV_B, batch - 1), g1 % NKV_B, i2

    def q_spec(f):
        return pl.BlockSpec((TOK, gw), lambda b, g, i: (f(b, g, i)[0] * bpb + 1 + f(b, g, i)[2],
                                                        P_QB // gw + f(b, g, i)[1]))

    def tab_spec(f):
        return pl.BlockSpec((TOK, HD_B), lambda b, g, i: (f(b, g, i)[2], 0))

    cur = lambda b, g, i: (b, g, i)
    return pl.pallas_call(
        functools.partial(_attn_kernel, tk=tk),
        out_shape=jax.ShapeDtypeStruct((batch * nch * TOK, Q_B), BF16),
        grid=(batch, NKV_B, nch),
        in_specs=[q_spec(cur), tab_spec(cur), tab_spec(cur), tab_spec(cur),
                  q_spec(nxt), tab_spec(nxt), tab_spec(nxt), tab_spec(nxt),
                  pl.BlockSpec((1, HD_B), lambda b, g, i: (0, 0)),
                  pl.BlockSpec((1, 1, tt, HD_B), lambda b, g, i: (b, g, 0, 0)),
                  pl.BlockSpec((1, 1, tk, HD_B), lambda b, g, i: (nxt(b, g, i)[0], nxt(b, g, i)[1], 0, 0)),
                  pl.BlockSpec((1, 1, HD_B, tt), lambda b, g, i: (b, g, 0, 0))],
        out_specs=pl.BlockSpec((TOK, gw), lambda b, g, i: (b * nch + i, g)),
        scratch_shapes=[pltpu.VMEM((nq, HD_B), BF16),
                        pltpu.VMEM((2, tk, nq), F32),
                        pltpu.VMEM((1, nq), F32),
                        pltpu.VMEM((1, nq), F32),
                        pltpu.VMEM((HD_B, nq), F32)],
        compiler_params=_params(("arbitrary", "arbitrary", "arbitrary")),
        name="attention",
    )(p, *rope, p, *rope, q_norm_w.reshape(1, HD_B), k_all, k_all, v_all)


def _branch_kernel(hn_ref, oa_ref, za_ref, at_ref, zb_ref, ga_ref, gb_ref, wa_ref, wb_ref, y_ref):
    a = _sigmoid(oa_ref[...].astype(F32)) * hn_ref[...].astype(F32) * _silu(za_ref[...].astype(F32))
    ya = _dot(a.astype(BF16), wa_ref[...])
    b = at_ref[...].astype(F32) * _silu(zb_ref[...].astype(F32))
    yb = _dot(b.astype(BF16), wb_ref[...])
    y = _sigmoid(ga_ref[...].astype(F32)) * ya + _sigmoid(gb_ref[...].astype(F32)) * yb
    y_ref[...] = y.astype(BF16)


def _branches(hn, o_attn, p, w_ba, w_bb, bpb):
    nl = hn.shape[0]
    nch = bpb - 1
    tm = TOK

    def col(k):
        return pl.BlockSpec((tm, D_MODEL), lambda i: ((i // nch) * bpb + 1 + i % nch, k // D_MODEL))

    own = pl.BlockSpec((tm, D_MODEL), lambda i: (i, 0))
    resident = pl.BlockSpec(memory_space=pltpu.VMEM)
    return pl.pallas_call(
        _branch_kernel,
        out_shape=jax.ShapeDtypeStruct((nl, D_MODEL), BF16),
        grid=(nl // tm,),
        in_specs=[own, col(P_OA), col(P_ZA), own, col(P_ZB), col(P_GA), col(P_GB), resident, resident],
        out_specs=own,
        compiler_params=_params(("arbitrary",)),
        name="branch_merge",
    )(hn, p, p, o_attn, p, p, p, w_ba, w_bb)


def _out_kernel(y_ref, x_ref, mod_ref, w_ref, lw_ref, lb_ref, o_ref, *, tiles_per_sample):
    b = pl.program_id(0) // tiles_per_sample
    gate = mod_ref[pl.ds(b, 1), 2 * D_MODEL:3 * D_MODEL]
    z = ALPHA * x_ref[...] + gate * _dot(y_ref[...], w_ref[...])
    mu = jnp.mean(z, axis=-1, keepdims=True)
    zc = z - mu
    var = jnp.mean(zc * zc, axis=-1, keepdims=True)
    o_ref[...] = zc * lax.rsqrt(var + EPS) * lw_ref[...] + lb_ref[...]


def _out_proj(y, x2, mod, w_out, ln_w, ln_b, seq):
    nl = y.shape[0]
    tm = _pick(seq, (512, 256))
    tile = pl.BlockSpec((tm, D_MODEL), lambda i: (i, 0))
    vec = pl.BlockSpec((1, D_MODEL), lambda i: (0, 0))
    return pl.pallas_call(
        functools.partial(_out_kernel, tiles_per_sample=seq // tm),
        out_shape=jax.ShapeDtypeStruct((nl, D_MODEL), F32),
        grid=(nl // tm,),
        in_specs=[tile, tile, pl.BlockSpec(mod.shape, lambda i: (0, 0)),
                  pl.BlockSpec(memory_space=pltpu.VMEM), vec, vec],
        out_specs=tile,
        compiler_params=_params(("arbitrary",)),
        name="out_proj",
    )(y, x2, mod, w_out, ln_w.reshape(1, -1), ln_b.reshape(1, -1))


def _rope_tables(seq):
    t = jnp.arange(seq)
    inv = ROPE_THETA ** (-jnp.arange(0, ROT_HALF, 2, dtype=F32) / ROT_HALF)
    ang_r = (t // GRID_W).astype(F32)[:, None] * inv[None]
    ang_c = (t % GRID_W).astype(F32)[:, None] * inv[None]
    cr, sr, cc, sc = jnp.cos(ang_r), jnp.sin(ang_r), jnp.cos(ang_c), jnp.sin(ang_c)
    z = jnp.zeros_like(sr)
    cos = jnp.concatenate([cr, cr, cc, cc], axis=1)
    sin_a = jnp.concatenate([-sr, z, -sc, z], axis=1)
    sin_b = jnp.concatenate([z, sr, z, sc], axis=1)
    return cos, sin_a, sin_b


def _block_out(x, c, ctx, c_ctx, w_mod, b_mod, w_in, b_if, conv_w, conv_b, mh_norm_w,
               q_norm_w, k_norm_w, w_ba, w_bb):
    batch, seq, d = x.shape
    assert d == D_MODEL and ctx.shape == (batch, TOK, D_MODEL) and seq % TOK == 0 and seq % GRID_W == 0
    assert w_in.shape == (D_MODEL, N_IN)
    bpb = seq // TOK + 1

    rows = -(-(batch + 1) // 8) * 8
    cc = jnp.zeros((rows, D_MODEL), F32).at[:batch].set(c).at[batch].set(c_ctx)
    mod = _modulation(cc, w_mod, b_mod)

    xc = jnp.concatenate([ctx, x], axis=1).reshape(batch * bpb * TOK, D_MODEL)
    w_p = jnp.concatenate([w_in[:, :W_IF], w_in[:, W_OA:], w_in[:, W_KB:W_OA]], axis=1).astype(BF16)
    w_if = jnp.pad(w_in[:, W_IF:W_KB], ((0, 0), (0, LANES - N_GATES))).astype(BF16)
    b_if_p = jnp.pad(b_if, (0, LANES - N_GATES)).reshape(1, LANES)
    p, gates = _in_proj(xc, mod, w_p, w_if, b_if_p, bpb, batch)

    rope = _rope_tables(seq)
    qk = _qk_conv(p, conv_w, conv_b, bpb)
    k_all, v_all = _kv_prep(p, rope, k_norm_w, batch, bpb)
    h_bwd = _mlstm_sweep(qk, p, gates, batch, bpb, reverse=True)
    hn = _mlstm_sweep(qk, p, gates, batch, bpb, reverse=False, h_bwd=h_bwd, norm_w=mh_norm_w)
    o_attn = _attention(p, k_all, v_all, rope, q_norm_w, batch, bpb)
    y = _branches(hn, o_attn, p, w_ba.astype(BF16), w_bb.astype(BF16), bpb)
    return y, mod


def kernel(x, c, ctx, c_ctx, w_mod, b_mod, w_in, b_if, conv_w, conv_b, mh_norm_w, q_norm_w, k_norm_w,
           w_branch_a, w_branch_b, w_out, ln_w, ln_b):
    assert w_mod.shape[0] == DEPTH
    batch, seq, _ = x.shape
    y, mod = _block_out(x, c, ctx, c_ctx, w_mod[0], b_mod[0], w_in[0], b_if[0], conv_w[0], conv_b[0],
                        mh_norm_w[0], q_norm_w[0], k_norm_w[0], w_branch_a[0], w_branch_b[0])
    out = _out_proj(y, x.reshape(batch * seq, D_MODEL), mod, w_out[0].astype(BF16), ln_w[0], ln_b[0], seq)
    return out.reshape(batch, seq, D_MODEL)
```

```python
import functools

import jax
import jax.numpy as jnp
from jax import lax
from jax.experimental import pallas as pl
from jax.experimental.pallas import tpu as pltpu

F32 = jnp.float32
BF16 = jnp.bfloat16

D_MODEL = 2048
GRID_W = 64
NH_A, DK_A, DV_A = 8, 128, 256
QK_A, V_A = NH_A * DK_A, NH_A * DV_A
N_GATES = 4 * NH_A
NH_B, NKV_B, HD_B = 16, 4, 128
GROUP_B = NH_B // NKV_B
Q_B, KV_B = NH_B * HD_B, NKV_B * HD_B
ROT_HALF = HD_B // 2
ROPE_THETA = 10000.0
EPS = 1e-6
DEPTH = 1
ALPHA = (2 * DEPTH) ** 0.25
M_INIT = -1e30
LOG2_E = 1.4426950408889634

LANES = 128
BF16_SUBLANES = 16
VMEM_LIMIT = 56 * 1024 * 1024

TOK = 256

P_QK, P_VA, P_OA, P_ZA, P_QB, P_ZB, P_GA, P_GB = (i * D_MODEL for i in range(8))
P_KB = 8 * D_MODEL
P_VB = P_KB + KV_B
N_P = P_VB + KV_B
W_IF = 2 * QK_A + V_A
W_KB = W_IF + N_GATES
W_OA = W_KB + 2 * KV_B
N_IN = W_OA + 2 * V_A + 2 * Q_B + 2 * D_MODEL


def _sigmoid(x):
    return 1.0 / (1.0 + jnp.exp(-x))


def _silu(x):
    return x * _sigmoid(x)


def _log_sigmoid(x):
    return jnp.minimum(x, 0.0) - jnp.log(1.0 + jnp.exp(-jnp.abs(x)))


def _dot(a, b):
    return jnp.dot(a, b, preferred_element_type=F32)


def _dot_nt(a, b):
    return lax.dot_general(a, b, (((1,), (1,)), ((), ())), preferred_element_type=F32)


def _dot_tn(a, b):
    return lax.dot_general(a, b, (((0,), (0,)), ((), ())), preferred_element_type=F32)


def _params(semantics):
    return pltpu.CompilerParams(dimension_semantics=semantics, vmem_limit_bytes=VMEM_LIMIT)


def _pick(n, candidates):
    for c in candidates:
        if n % c == 0:
            return c
    raise ValueError(f"no tile in {candidates} divides {n}")


def _mod_kernel(c_ref, w_ref, b_ref, o_ref):
    a = _silu(c_ref[...]).astype(BF16)
    o_ref[...] = _dot(a, w_ref[...].astype(BF16)) + b_ref[...]


def _modulation(cc, w_mod, b_mod):
    rows = cc.shape[0]
    n = w_mod.shape[1]
    tn = _pick(n, (768, 512, 256, 128))
    return pl.pallas_call(
        _mod_kernel,
        out_shape=jax.ShapeDtypeStruct((rows, n), F32),
        grid=(n // tn,),
        in_specs=[pl.BlockSpec((rows, D_MODEL), lambda j: (0, 0)),
                  pl.BlockSpec((D_MODEL, tn), lambda j: (0, j)),
                  pl.BlockSpec((1, tn), lambda j: (0, j))],
        out_specs=pl.BlockSpec((rows, tn), lambda j: (0, j)),
        compiler_params=_params(("arbitrary",)),
        name="modulation",
    )(cc, w_mod, b_mod.reshape(1, n))


def _in_proj_kernel(x_ref, mod_ref, w_ref, wif_ref, bif_ref, p_ref, g_ref, u_ref, *, tm, bpb, ctx_row):
    i = pl.program_id(0)
    j = pl.program_id(1)

    @pl.when(j == 0)
    def _():
        for s in range(tm // TOK):
            t = i * (tm // TOK) + s
            row = jnp.where(t % bpb == 0, ctx_row, t // bpb)
            xs = x_ref[s * TOK:(s + 1) * TOK, :]
            mu = jnp.mean(xs, axis=-1, keepdims=True)
            xc = xs - mu
            var = jnp.mean(xc * xc, axis=-1, keepdims=True)
            y = xc * lax.rsqrt(var + EPS)
            shift = mod_ref[pl.ds(row, 1), 0:D_MODEL]
            scale = mod_ref[pl.ds(row, 1), D_MODEL:2 * D_MODEL]
            u_ref[s * TOK:(s + 1) * TOK, :] = (y * (1.0 + scale) + shift).astype(BF16)
        g_ref[...] = _dot(u_ref[...], wif_ref[...]) + bif_ref[...]

    p_ref[...] = _dot(u_ref[...], w_ref[...]).astype(BF16)


def _in_proj(xc, mod, w_p, w_if, b_if, bpb, ctx_row):
    nt = xc.shape[0]
    tm = _pick(nt, (1024, 512, 256))
    tn = 1024
    kern = functools.partial(_in_proj_kernel, tm=tm, bpb=bpb, ctx_row=ctx_row)
    return pl.pallas_call(
        kern,
        out_shape=(jax.ShapeDtypeStruct((nt, N_P), BF16), jax.ShapeDtypeStruct((nt, LANES), F32)),
        grid=(nt // tm, N_P // tn),
        in_specs=[pl.BlockSpec((tm, D_MODEL), lambda i, j: (i, 0)),
                  pl.BlockSpec(mod.shape, lambda i, j: (0, 0)),
                  pl.BlockSpec((D_MODEL, tn), lambda i, j: (0, j)),
                  pl.BlockSpec((D_MODEL, LANES), lambda i, j: (0, 0)),
                  pl.BlockSpec((1, LANES), lambda i, j: (0, 0))],
        out_specs=(pl.BlockSpec((tm, tn), lambda i, j: (i, j)),
                   pl.BlockSpec((tm, LANES), lambda i, j: (i, 0))),
        scratch_shapes=[pltpu.VMEM((tm, D_MODEL), BF16)],
        compiler_params=_params(("arbitrary", "arbitrary")),
        name="in_proj",
    )(xc, mod, w_p, w_if, b_if)


def _qk_conv_kernel(x_ref, prev_ref, next_ref, cw_ref, cb_ref, o_ref, *, bpb):
    tb = pl.program_id(0) % bpb
    has_prev = (tb >= 2).astype(F32)
    has_next = jnp.logical_and(tb >= 1, tb < bpb - 1).astype(F32)
    x = x_ref[...].astype(F32)
    prev_row = prev_ref[BF16_SUBLANES - 1:BF16_SUBLANES, :].astype(F32) * has_prev
    next_row = next_ref[0:1, :].astype(F32) * has_next
    rows = lax.broadcasted_iota(jnp.int32, (TOK, 1), 0)
    x_prev = jnp.where(rows == 0, prev_row, pltpu.roll(x, 1, axis=0))
    x_next = jnp.where(rows == TOK - 1, next_row, pltpu.roll(x, TOK - 1, axis=0))
    y = cb_ref[...] + x_prev * cw_ref[0:1, :] + x * cw_ref[1:2, :] + x_next * cw_ref[2:3, :]
    qk = _silu(y)
    o_ref[:, 0:QK_A] = qk[:, 0:QK_A].astype(BF16)
    o_ref[:, QK_A:2 * QK_A] = (qk[:, QK_A:2 * QK_A] * DK_A ** -0.5).astype(BF16)


def _qk_conv(p, conv_w, conv_b, bpb):
    nt = p.shape[0]
    nblk = nt // TOK
    sub = TOK // BF16_SUBLANES
    last = nt // BF16_SUBLANES - 1
    return pl.pallas_call(
        functools.partial(_qk_conv_kernel, bpb=bpb),
        out_shape=jax.ShapeDtypeStruct((nt, 2 * QK_A), BF16),
        grid=(nblk,),
        in_specs=[pl.BlockSpec((TOK, 2 * QK_A), lambda t: (t, P_QK // (2 * QK_A))),
                  pl.BlockSpec((BF16_SUBLANES, 2 * QK_A), lambda t: (jnp.maximum(t * sub - 1, 0), 0)),
                  pl.BlockSpec((BF16_SUBLANES, 2 * QK_A), lambda t: (jnp.minimum((t + 1) * sub, last), 0)),
                  pl.BlockSpec((3, 2 * QK_A), lambda t: (0, 0)),
                  pl.BlockSpec((1, 2 * QK_A), lambda t: (0, 0))],
        out_specs=pl.BlockSpec((TOK, 2 * QK_A), lambda t: (t, 0)),
        compiler_params=_params(("arbitrary",)),
        name="mlstm_qk_conv",
    )(p, p, p, conv_w, conv_b.reshape(1, -1))


def _rope(x, cos, sin_a, sin_b):
    half = ROT_HALF // 2
    return x * cos + pltpu.roll(x, HD_B - half, axis=1) * sin_a + pltpu.roll(x, half, axis=1) * sin_b


def _rms(x, w):
    return x * lax.rsqrt(jnp.mean(x * x, axis=-1, keepdims=True) + EPS) * w


def _kv_prep_kernel(k_ref, v_ref, cos_ref, sa_ref, sb_ref, kw_ref, ko_ref, vo_ref, *, bpb):
    is_lat = pl.program_id(0) % bpb > 0
    cos = jnp.where(is_lat, cos_ref[...], 1.0)
    sa = jnp.where(is_lat, sa_ref[...], 0.0)
    sb = jnp.where(is_lat, sb_ref[...], 0.0)
    for h in range(NKV_B):
        kh = k_ref[:, h * HD_B:(h + 1) * HD_B].astype(F32)
        ko_ref[0, h] = _rope(_rms(kh, kw_ref[...]), cos, sa, sb).astype(BF16)
        vo_ref[0, h] = v_ref[:, h * HD_B:(h + 1) * HD_B].astype(F32).T.astype(BF16)


def _kv_prep(p, rope, k_norm_w, batch, bpb):
    nt = p.shape[0]
    tt = bpb * TOK
    tab = pl.BlockSpec((TOK, HD_B), lambda t: (jnp.maximum(t % bpb - 1, 0), 0))
    return pl.pallas_call(
        functools.partial(_kv_prep_kernel, bpb=bpb),
        out_shape=(jax.ShapeDtypeStruct((batch, NKV_B, tt, HD_B), BF16),
                   jax.ShapeDtypeStruct((batch, NKV_B, HD_B, tt), BF16)),
        grid=(nt // TOK,),
        in_specs=[pl.BlockSpec((TOK, KV_B), lambda t: (t, P_KB // KV_B)),
                  pl.BlockSpec((TOK, KV_B), lambda t: (t, P_VB // KV_B)),
                  tab, tab, tab,
                  pl.BlockSpec((1, HD_B), lambda t: (0, 0))],
        out_specs=(pl.BlockSpec((1, NKV_B, TOK, HD_B), lambda t: (t // bpb, 0, t % bpb, 0)),
                   pl.BlockSpec((1, NKV_B, HD_B, TOK), lambda t: (t // bpb, 0, 0, t % bpb))),
        compiler_params=_params(("arbitrary",)),
        name="attn_kv_prep",
    )(p, p, *rope, k_norm_w.reshape(1, HD_B))


def _mlstm_kernel(*refs, reverse, fuse):
    if fuse:
        q_ref, k_ref, v_ref, g_ref, hb_ref, nw_ref, h_ref, c_ref, m_ref = refs
    else:
        q_ref, k_ref, v_ref, g_ref, h_ref, c_ref, m_ref = refs
    L = TOK

    @pl.when(pl.program_id(1) == 0)
    def _():
        c_ref[...] = jnp.zeros_like(c_ref)
        m_ref[...] = jnp.full_like(m_ref, M_INIT)

    rows = lax.broadcasted_iota(jnp.int32, (L, L), 0)
    cols = lax.broadcasted_iota(jnp.int32, (L, L), 1)
    mask = (cols >= rows) if reverse else (cols <= rows)
    tri = mask.astype(BF16)

    g = g_ref[...]
    lf = _log_sigmoid(g)
    lf1 = lf.astype(BF16)
    r1 = lf - lf1.astype(F32)
    lf2 = r1.astype(BF16)
    lf3 = (r1 - lf2.astype(F32)).astype(BF16)
    bsum = _dot(tri, lf1) + _dot(tri, lf2) + _dot(tri, lf3)
    bal = pltpu.roll(bsum, LANES - NH_A, axis=1)
    ib = g - bal
    ib_t = ib.T
    end = 0 if reverse else L - 1
    lane0 = 2 * NH_A if reverse else 0

    rowi = lax.broadcasted_iota(jnp.int32, (L, LANES), 0)
    cm = ib
    k = 1
    while k < L:
        if reverse:
            shifted, ok = pltpu.roll(cm, L - k, axis=0), rowi < L - k
        else:
            shifted, ok = pltpu.roll(cm, k, axis=0), rowi >= k
        cm = jnp.maximum(cm, jnp.where(ok, shifted, -jnp.inf))
        k *= 2
    m0 = m_ref[...]
    mx = jnp.maximum(cm, m0)
    a_all = jnp.exp(m0 - mx)
    m_all = bal + mx
    en_all = jnp.exp(-m_all)
    m_end = m_all[end:end + 1, :]
    a_end_all = a_all[end:end + 1, :]
    wend_all = jnp.exp(bal[end:end + 1, :] + ib - m_end)
    m_ref[...] = m_end
    ones_blk = (lax.broadcasted_iota(jnp.int32, (L, LANES), 1) == 0).astype(BF16)
    ones_dv = jnp.ones((DV_A, LANES), BF16)

    for h in range(NH_A):
        li = lane0 + h
        ibrow = ib_t[li:li + 1, :]
        w = jnp.where(mask, jnp.exp(ibrow - mx[:, li:li + 1]), 0.0)
        a = a_all[:, li:li + 1]
        w_end = wend_all[:, li:li + 1]
        a_end = a_end_all[:, li:li + 1]
        qh = q_ref[:, h * DK_A:(h + 1) * DK_A]
        kh = k_ref[:, h * DK_A:(h + 1) * DK_A]
        vh = v_ref[:, h * DV_A:(h + 1) * DV_A]
        s = _dot_nt(qh, kh) * w
        st0 = c_ref[h]
        qc = _dot_nt(qh, st0.astype(BF16))
        sv = _dot(s.astype(BF16), jnp.concatenate([vh, ones_blk], axis=1))
        num = a * qc[:, 0:DV_A] + sv[:, 0:DV_A]
        den = a * qc[:, DV_A:DV_A + 1] + sv[:, DV_A:DV_A + 1]
        hh = num / jnp.maximum(jnp.abs(den), en_all[:, li:li + 1])
        if fuse:
            tot = hh + hb_ref[:, h * DV_A:(h + 1) * DV_A].astype(F32)
            sq = tot * tot
            sq_hi = sq.astype(BF16)
            sq_lo = (sq - sq_hi.astype(F32)).astype(BF16)
            ssq = _dot(sq_hi, ones_dv) + _dot(sq_lo, ones_dv)
            rs = lax.rsqrt(ssq * (1.0 / DV_A) + EPS)
            for t in range(DV_A // LANES):
                sl = slice(h * DV_A + t * LANES, h * DV_A + (t + 1) * LANES)
                h_ref[:, sl] = (tot[:, t * LANES:(t + 1) * LANES] * rs * nw_ref[:, sl]).astype(h_ref.dtype)
        else:
            h_ref[:, h * DV_A:(h + 1) * DV_A] = hh.astype(h_ref.dtype)

        vw = (vh.astype(F32) * w_end).astype(BF16)
        c_ref[h, 0:DV_A, :] = a_end * st0[0:DV_A, :] + _dot_tn(vw, kh)
        c_ref[h, DV_A:DV_A + 1, :] = (a_end * st0[DV_A:DV_A + 1, :]
                                      + jnp.sum(kh.astype(F32) * w_end, axis=0, keepdims=True))


def _mlstm_sweep(qk, p, gates, batch, bpb, reverse, h_bwd=None, norm_w=None):
    nch = bpb - 1
    fuse = h_bwd is not None

    def src(b, c):
        if reverse:
            return b * bpb + jnp.where(c == 0, 0, bpb - c)
        return b * bpb + c

    def dst(b, c):
        if reverse:
            return b * nch + nch - jnp.maximum(c, 1)
        return b * nch + jnp.maximum(c, 1) - 1

    in_specs = [pl.BlockSpec((TOK, QK_A), lambda b, c: (src(b, c), 0)),
                pl.BlockSpec((TOK, QK_A), lambda b, c: (src(b, c), 1)),
                pl.BlockSpec((TOK, V_A), lambda b, c: (src(b, c), P_VA // V_A)),
                pl.BlockSpec((TOK, LANES), lambda b, c: (src(b, c), 0))]
    args = [qk, qk, p, gates]
    if fuse:
        in_specs += [pl.BlockSpec((TOK, V_A), lambda b, c: (dst(b, c), 0)),
                     pl.BlockSpec((1, V_A), lambda b, c: (0, 0))]
        args += [h_bwd, norm_w.reshape(1, V_A)]
    return pl.pallas_call(
        functools.partial(_mlstm_kernel, reverse=reverse, fuse=fuse),
        out_shape=jax.ShapeDtypeStruct((batch * nch * TOK, V_A), BF16),
        grid=(batch, bpb),
        in_specs=in_specs,
        out_specs=pl.BlockSpec((TOK, V_A), lambda b, c: (dst(b, c), 0)),
        scratch_shapes=[pltpu.VMEM((NH_A, DV_A + LANES, DK_A), F32),
                        pltpu.VMEM((1, LANES), F32)],
        compiler_params=_params(("arbitrary", "arbitrary")),
        name="mlstm_bwd" if reverse else "mlstm_fwd",
    )(*args)


ATTN_UNROLL = 4


def _attn_kernel(q_ref, cos_ref, sa_ref, sb_ref, qn_ref, cosn_ref, san_ref, sbn_ref, qw_ref,
                 k_ref, kn_ref, vt_ref, o_ref, qs_ref, s_ref, m_ref, l_ref, acc_ref, *, tk):
    tq = TOK
    n_kv = k_ref.shape[2] // tk
    assert n_kv % 2 == 0

    def prep(qr_ref, c_ref, a_ref, b_ref):
        cos, sa, sb = c_ref[...], a_ref[...], b_ref[...]
        for h in range(GROUP_B):
            qh = qr_ref[:, h * HD_B:(h + 1) * HD_B].astype(F32)
            qr = _rope(_rms(qh, qw_ref[...]), cos, sa, sb)
            qs_ref[h * tq:(h + 1) * tq, :] = (qr * (HD_B ** -0.5 * LOG2_E)).astype(BF16)

    def scores(c, slot):
        off = pl.multiple_of(c * tk, tk)
        s_ref[slot] = _dot_nt(k_ref[0, 0, pl.ds(off, tk), :], qs_ref[...])

    def consume(c, slot):
        off = pl.multiple_of(c * tk, LANES)
        s = s_ref[slot]
        m_old = m_ref[...]
        m_new = jnp.maximum(m_old, jnp.max(s, axis=0, keepdims=True))
        pexp = jnp.exp2(s - m_new)
        alpha = jnp.exp2(m_old - m_new)
        l_ref[...] = alpha * l_ref[...] + jnp.sum(pexp, axis=0, keepdims=True)
        acc_ref[...] = alpha * acc_ref[...] + _dot(vt_ref[0, 0, :, pl.ds(off, tk)], pexp.astype(BF16))
        m_ref[...] = m_new

    first = jnp.logical_and(pl.program_id(0) == 0,
                            jnp.logical_and(pl.program_id(1) == 0, pl.program_id(2) == 0))

    @pl.when(first)
    def _():
        prep(q_ref, cos_ref, sa_ref, sb_ref)
        scores(0, 0)

    m_ref[...] = jnp.full_like(m_ref, -jnp.inf)
    l_ref[...] = jnp.zeros_like(l_ref)
    acc_ref[...] = jnp.zeros_like(acc_ref)

    def trip(t, carry):
        for u in range(ATTN_UNROLL):
            c = ATTN_UNROLL * t + u
            scores(c + 1, (u + 1) % 2)
            consume(c, u % 2)
        return carry

    n_trips = (n_kv - 2) // ATTN_UNROLL
    lax.fori_loop(0, n_trips, trip, 0)
    for c in range(ATTN_UNROLL * n_trips, n_kv - 1):
        scores(c + 1, (c + 1) % 2)
        consume(c, c % 2)
    prep(qn_ref, cosn_ref, san_ref, sbn_ref)
    s_ref[0] = _dot_nt(kn_ref[0, 0], qs_ref[...])
    consume(n_kv - 1, 1)

    o_t = acc_ref[...] / l_ref[...]
    for h in range(GROUP_B):
        o_ref[:, h * HD_B:(h + 1) * HD_B] = o_t[:, h * tq:(h + 1) * tq].T.astype(BF16)


def _attention(p, k_all, v_all, rope, q_norm_w, batch, bpb):
    nch = bpb - 1
    tt = bpb * TOK
    tk = next(c for c in (384, 256, 128) if tt % c == 0 and (tt // c) % 2 == 0)
    gw = GROUP_B * HD_B
    nq = GROUP_B * TOK

    def nxt(b, g, i):
        i2 = (i + 1) % nch
        g1 = g + (i + 1) // nch
        return jnp.minimum(b + g1 // NKV_B, batch - 1), g1 % NKV_B, i2

    def q_spec(f):
        return pl.BlockSpec((TOK, gw), lambda b, g, i: (f(b, g, i)[0] * bpb + 1 + f(b, g, i)[2],
                                                        P_QB // gw + f(b, g, i)[1]))

    def tab_spec(f):
        return pl.BlockSpec((TOK, HD_B), lambda b, g, i: (f(b, g, i)[2], 0))

    cur = lambda b, g, i: (b, g, i)
    return pl.pallas_call(
        functools.partial(_attn_kernel, tk=tk),
        out_shape=jax.ShapeDtypeStruct((batch * nch * TOK, Q_B), BF16),
        grid=(batch, NKV_B, nch),
        in_specs=[q_spec(cur), tab_spec(cur), tab_spec(cur), tab_spec(cur),
                  q_spec(nxt), tab_spec(nxt), tab_spec(nxt), tab_spec(nxt),
                  pl.BlockSpec((1, HD_B), lambda b, g, i: (0, 0)),
                  pl.BlockSpec((1, 1, tt, HD_B), lambda b, g, i: (b, g, 0, 0)),
                  pl.BlockSpec((1, 1, tk, HD_B), lambda b, g, i: (nxt(b, g, i)[0], nxt(b, g, i)[1], 0, 0)),
                  pl.BlockSpec((1, 1, HD_B, tt), lambda b, g, i: (b, g, 0, 0))],
        out_specs=pl.BlockSpec((TOK, gw), lambda b, g, i: (b * nch + i, g)),
        scratch_shapes=[pltpu.VMEM((nq, HD_B), BF16),
                        pltpu.VMEM((2, tk, nq), F32),
                        pltpu.VMEM((1, nq), F32),
                        pltpu.VMEM((1, nq), F32),
                        pltpu.VMEM((HD_B, nq), F32)],
        compiler_params=_params(("arbitrary", "arbitrary", "arbitrary")),
        name="attention",
    )(p, *rope, p, *rope, q_norm_w.reshape(1, HD_B), k_all, k_all, v_all)


def _branch_kernel(hn_ref, oa_ref, za_ref, at_ref, zb_ref, ga_ref, gb_ref, wa_ref, wb_ref, y_ref):
    a = _sigmoid(oa_ref[...].astype(F32)) * hn_ref[...].astype(F32) * _silu(za_ref[...].astype(F32))
    ya = _dot(a.astype(BF16), wa_ref[...])
    b = at_ref[...].astype(F32) * _silu(zb_ref[...].astype(F32))
    yb = _dot(b.astype(BF16), wb_ref[...])
    y = _sigmoid(ga_ref[...].astype(F32)) * ya + _sigmoid(gb_ref[...].astype(F32)) * yb
    y_ref[...] = y.astype(BF16)


def _branches(hn, o_attn, p, w_ba, w_bb, bpb):
    nl = hn.shape[0]
    nch = bpb - 1
    tm = TOK

    def col(k):
        return pl.BlockSpec((tm, D_MODEL), lambda i: ((i // nch) * bpb + 1 + i % nch, k // D_MODEL))

    own = pl.BlockSpec((tm, D_MODEL), lambda i: (i, 0))
    resident = pl.BlockSpec(memory_space=pltpu.VMEM)
    return pl.pallas_call(
        _branch_kernel,
        out_shape=jax.ShapeDtypeStruct((nl, D_MODEL), BF16),
        grid=(nl // tm,),
        in_specs=[own, col(P_OA), col(P_ZA), own, col(P_ZB), col(P_GA), col(P_GB), resident, resident],
        out_specs=own,
        compiler_params=_params(("arbitrary",)),
        name="branch_merge",
    )(hn, p, p, o_attn, p, p, p, w_ba, w_bb)


def _out_kernel(y_ref, x_ref, mod_ref, w_ref, lw_ref, lb_ref, o_ref, *, tiles_per_sample):
    b = pl.program_id(0) // tiles_per_sample
    gate = mod_ref[pl.ds(b, 1), 2 * D_MODEL:3 * D_MODEL]
    z = ALPHA * x_ref[...] + gate * _dot(y_ref[...], w_ref[...])
    mu = jnp.mean(z, axis=-1, keepdims=True)
    zc = z - mu
    var = jnp.mean(zc * zc, axis=-1, keepdims=True)
    o_ref[...] = zc * lax.rsqrt(var + EPS) * lw_ref[...] + lb_ref[...]


def _out_proj(y, x2, mod, w_out, ln_w, ln_b, seq):
    nl = y.shape[0]
    tm = _pick(seq, (512, 256))
    tile = pl.BlockSpec((tm, D_MODEL), lambda i: (i, 0))
    vec = pl.BlockSpec((1, D_MODEL), lambda i: (0, 0))
    return pl.pallas_call(
        functools.partial(_out_kernel, tiles_per_sample=seq // tm),
        out_shape=jax.ShapeDtypeStruct((nl, D_MODEL), F32),
        grid=(nl // tm,),
        in_specs=[tile, tile, pl.BlockSpec(mod.shape, lambda i: (0, 0)),
                  pl.BlockSpec(memory_space=pltpu.VMEM), vec, vec],
        out_specs=tile,
        compiler_params=_params(("arbitrary",)),
        name="out_proj",
    )(y, x2, mod, w_out, ln_w.reshape(1, -1), ln_b.reshape(1, -1))


def _rope_tables(seq):
    t = jnp.arange(seq)
    inv = ROPE_THETA ** (-jnp.arange(0, ROT_HALF, 2, dtype=F32) / ROT_HALF)
    ang_r = (t // GRID_W).astype(F32)[:, None] * inv[None]
    ang_c = (t % GRID_W).astype(F32)[:, None] * inv[None]
    cr, sr, cc, sc = jnp.cos(ang_r), jnp.sin(ang_r), jnp.cos(ang_c), jnp.sin(ang_c)
    z = jnp.zeros_like(sr)
    cos = jnp.concatenate([cr, cr, cc, cc], axis=1)
    sin_a = jnp.concatenate([-sr, z, -sc, z], axis=1)
    sin_b = jnp.concatenate([z, sr, z, sc], axis=1)
    return cos, sin_a, sin_b


def _block_out(x, c, ctx, c_ctx, w_mod, b_mod, w_in, b_if, conv_w, conv_b, mh_norm_w,
               q_norm_w, k_norm_w, w_ba, w_bb):
    batch, seq, d = x.shape
    assert d == D_MODEL and ctx.shape == (batch, TOK, D_MODEL) and seq % TOK == 0 and seq % GRID_W == 0
    assert w_in.shape == (D_MODEL, N_IN)
    bpb = seq // TOK + 1

    rows = -(-(batch + 1) // 8) * 8
    cc = jnp.zeros((rows, D_MODEL), F32).at[:batch].set(c).at[batch].set(c_ctx)
    mod = _modulation(cc, w_mod, b_mod)

    xc = jnp.concatenate([ctx, x], axis=1).reshape(batch * bpb * TOK, D_MODEL)
    w_p = jnp.concatenate([w_in[:, :W_IF], w_in[:, W_OA:], w_in[:, W_KB:W_OA]], axis=1).astype(BF16)
    w_if = jnp.pad(w_in[:, W_IF:W_KB], ((0, 0), (0, LANES - N_GATES))).astype(BF16)
    b_if_p = jnp.pad(b_if, (0, LANES - N_GATES)).reshape(1, LANES)
    p, gates = _in_proj(xc, mod, w_p, w_if, b_if_p, bpb, batch)

    rope = _rope_tables(seq)
    qk = _qk_conv(p, conv_w, conv_b, bpb)
    k_all, v_all = _kv_prep(p, rope, k_norm_w, batch, bpb)
    h_bwd = _mlstm_sweep(qk, p, gates, batch, bpb, reverse=True)
    hn = _mlstm_sweep(qk, p, gates, batch, bpb, reverse=False, h_bwd=h_bwd, norm_w=mh_norm_w)
    o_attn = _attention(p, k_all, v_all, rope, q_norm_w, batch, bpb)
    y = _branches(hn, o_attn, p, w_ba.astype(BF16), w_bb.astype(BF16), bpb)
    return y, mod


def kernel(x, c, ctx, c_ctx, w_mod, b_mod, w_in, b_if, conv_w, conv_b, mh_norm_w, q_norm_w, k_norm_w,
           w_branch_a, w_branch_b, w_out, ln_w, ln_b):
    assert w_mod.shape[0] == DEPTH
    batch, seq, _ = x.shape
    y, mod = _block_out(x, c, ctx, c_ctx, w_mod[0], b_mod[0], w_in[0], b_if[0], conv_w[0], conv_b[0],
                        mh_norm_w[0], q_norm_w[0], k_norm_w[0], w_branch_a[0], w_branch_b[0])
    out = _out_proj(y, x.reshape(batch * seq, D_MODEL), mod, w_out[0].astype(BF16), ln_w[0], ln_b[0], seq)
    return out.reshape(batch, seq, D_MODEL)
```

```python
import functools

import jax
import jax.numpy as jnp
from jax import lax
from jax.experimental import pallas as pl
from jax.experimental.pallas import tpu as pltpu

F32 = jnp.float32
BF16 = jnp.bfloat16

D_MODEL = 2048
GRID_W = 64
NH_A, DK_A, DV_A = 8, 128, 256
QK_A, V_A = NH_A * DK_A, NH_A * DV_A
N_GATES = 4 * NH_A
NH_B, NKV_B, HD_B = 16, 4, 128
GROUP_B = NH_B // NKV_B
Q_B, KV_B = NH_B * HD_B, NKV_B * HD_B
ROT_HALF = HD_B // 2
ROPE_THETA = 10000.0
EPS = 1e-6
DEPTH = 1
ALPHA = (2 * DEPTH) ** 0.25
M_INIT = -1e30
LOG2_E = 1.4426950408889634

LANES = 128
BF16_SUBLANES = 16
VMEM_LIMIT = 56 * 1024 * 1024

TOK = 256

P_QK, P_VA, P_OA, P_ZA, P_QB, P_ZB, P_GA, P_GB = (i * D_MODEL for i in range(8))
P_KB = 8 * D_MODEL
P_VB = P_KB + KV_B
N_P = P_VB + KV_B
W_IF = 2 * QK_A + V_A
W_KB = W_IF + N_GATES
W_OA = W_KB + 2 * KV_B
N_IN = W_OA + 2 * V_A + 2 * Q_B + 2 * D_MODEL


def _sigmoid(x):
    return 1.0 / (1.0 + jnp.exp(-x))


def _silu(x):
    return x * _sigmoid(x)


def _log_sigmoid(x):
    return jnp.minimum(x, 0.0) - jnp.log(1.0 + jnp.exp(-jnp.abs(x)))


def _dot(a, b):
    return jnp.dot(a, b, preferred_element_type=F32)


def _dot_nt(a, b):
    return lax.dot_general(a, b, (((1,), (1,)), ((), ())), preferred_element_type=F32)


def _dot_tn(a, b):
    return lax.dot_general(a, b, (((0,), (0,)), ((), ())), preferred_element_type=F32)


def _params(semantics):
    return pltpu.CompilerParams(dimension_semantics=semantics, vmem_limit_bytes=VMEM_LIMIT)


def _pick(n, candidates):
    for c in candidates:
        if n % c == 0:
            return c
    raise ValueError(f"no tile in {candidates} divides {n}")


def _mod_kernel(c_ref, w_ref, b_ref, o_ref):
    a = _silu(c_ref[...]).astype(BF16)
    o_ref[...] = _dot(a, w_ref[...].astype(BF16)) + b_ref[...]


def _modulation(cc, w_mod, b_mod):
    rows = cc.shape[0]
    n = w_mod.shape[1]
    tn = _pick(n, (768, 512, 256, 128))
    return pl.pallas_call(
        _mod_kernel,
        out_shape=jax.ShapeDtypeStruct((rows, n), F32),
        grid=(n // tn,),
        in_specs=[pl.BlockSpec((rows, D_MODEL), lambda j: (0, 0)),
                  pl.BlockSpec((D_MODEL, tn), lambda j: (0, j)),
                  pl.BlockSpec((1, tn), lambda j: (0, j))],
        out_specs=pl.BlockSpec((rows, tn), lambda j: (0, j)),
        compiler_params=_params(("arbitrary",)),
        name="modulation",
    )(cc, w_mod, b_mod.reshape(1, n))


def _in_proj_kernel(x_ref, mod_ref, w_ref, wif_ref, bif_ref, p_ref, g_ref, u_ref, *, tm, bpb, ctx_row):
    i = pl.program_id(0)
    j = pl.program_id(1)

    @pl.when(j == 0)
    def _():
        for s in range(tm // TOK):
            t = i * (tm // TOK) + s
            row = jnp.where(t % bpb == 0, ctx_row, t // bpb)
            xs = x_ref[s * TOK:(s + 1) * TOK, :]
            mu = jnp.mean(xs, axis=-1, keepdims=True)
            xc = xs - mu
            var = jnp.mean(xc * xc, axis=-1, keepdims=True)
            y = xc * lax.rsqrt(var + EPS)
            shift = mod_ref[pl.ds(row, 1), 0:D_MODEL]
            scale = mod_ref[pl.ds(row, 1), D_MODEL:2 * D_MODEL]
            u_ref[s * TOK:(s + 1) * TOK, :] = (y * (1.0 + scale) + shift).astype(BF16)
        g_ref[...] = _dot(u_ref[...], wif_ref[...]) + bif_ref[...]

    p_ref[...] = _dot(u_ref[...], w_ref[...]).astype(BF16)


def _in_proj(xc, mod, w_p, w_if, b_if, bpb, ctx_row):
    nt = xc.shape[0]
    tm = _pick(nt, (1024, 512, 256))
    tn = 1024
    kern = functools.partial(_in_proj_kernel, tm=tm, bpb=bpb, ctx_row=ctx_row)
    return pl.pallas_call(
        kern,
        out_shape=(jax.ShapeDtypeStruct((nt, N_P), BF16), jax.ShapeDtypeStruct((nt, LANES), F32)),
        grid=(nt // tm, N_P // tn),
        in_specs=[pl.BlockSpec((tm, D_MODEL), lambda i, j: (i, 0)),
                  pl.BlockSpec(mod.shape, lambda i, j: (0, 0)),
                  pl.BlockSpec((D_MODEL, tn), lambda i, j: (0, j)),
                  pl.BlockSpec((D_MODEL, LANES), lambda i, j: (0, 0)),
                  pl.BlockSpec((1, LANES), lambda i, j: (0, 0))],
        out_specs=(pl.BlockSpec((tm, tn), lambda i, j: (i, j)),
                   pl.BlockSpec((tm, LANES), lambda i, j: (i, 0))),
        scratch_shapes=[pltpu.VMEM((tm, D_MODEL), BF16)],
        compiler_params=_params(("arbitrary", "arbitrary")),
        name="in_proj",
    )(xc, mod, w_p, w_if, b_if)


def _qk_conv_kernel(x_ref, prev_ref, next_ref, cw_ref, cb_ref, o_ref, *, bpb):
    tb = pl.program_id(0) % bpb
    has_prev = (tb >= 2).astype(F32)
    has_next = jnp.logical_and(tb >= 1, tb < bpb - 1).astype(F32)
    x = x_ref[...].astype(F32)
    prev_row = prev_ref[BF16_SUBLANES - 1:BF16_SUBLANES, :].astype(F32) * has_prev
    next_row = next_ref[0:1, :].astype(F32) * has_next
    rows = lax.broadcasted_iota(jnp.int32, (TOK, 1), 0)
    x_prev = jnp.where(rows == 0, prev_row, pltpu.roll(x, 1, axis=0))
    x_next = jnp.where(rows == TOK - 1, next_row, pltpu.roll(x, TOK - 1, axis=0))
    y = cb_ref[...] + x_prev * cw_ref[0:1, :] + x * cw_ref[1:2, :] + x_next * cw_ref[2:3, :]
    qk = _silu(y)
    o_ref[:, 0:QK_A] = qk[:, 0:QK_A].astype(BF16)
    o_ref[:, QK_A:2 * QK_A] = (qk[:, QK_A:2 * QK_A] * DK_A ** -0.5).astype(BF16)


def _qk_conv(p, conv_w, conv_b, bpb):
    nt = p.shape[0]
    nblk = nt // TOK
    sub = TOK // BF16_SUBLANES
    last = nt // BF16_SUBLANES - 1
    return pl.pallas_call(
        functools.partial(_qk_conv_kernel, bpb=bpb),
        out_shape=jax.ShapeDtypeStruct((nt, 2 * QK_A), BF16),
        grid=(nblk,),
        in_specs=[pl.BlockSpec((TOK, 2 * QK_A), lambda t: (t, P_QK // (2 * QK_A))),
                  pl.BlockSpec((BF16_SUBLANES, 2 * QK_A), lambda t: (jnp.maximum(t * sub - 1, 0), 0)),
                  pl.BlockSpec((BF16_SUBLANES, 2 * QK_A), lambda t: (jnp.minimum((t + 1) * sub, last), 0)),
                  pl.BlockSpec((3, 2 * QK_A), lambda t: (0, 0)),
                  pl.BlockSpec((1, 2 * QK_A), lambda t: (0, 0))],
        out_specs=pl.BlockSpec((TOK, 2 * QK_A), lambda t: (t, 0)),
        compiler_params=_params(("arbitrary",)),
        name="mlstm_qk_conv",
    )(p, p, p, conv_w, conv_b.reshape(1, -1))


def _rope(x, cos, sin_a, sin_b):
    half = ROT_HALF // 2
    return x * cos + pltpu.roll(x, HD_B - half, axis=1) * sin_a + pltpu.roll(x, half, axis=1) * sin_b


def _rms(x, w):
    return x * lax.rsqrt(jnp.mean(x * x, axis=-1, keepdims=True) + EPS) * w


def _kv_prep_kernel(k_ref, v_ref, cos_ref, sa_ref, sb_ref, kw_ref, ko_ref, vo_ref, *, bpb):
    is_lat = pl.program_id(0) % bpb > 0
    cos = jnp.where(is_lat, cos_ref[...], 1.0)
    sa = jnp.where(is_lat, sa_ref[...], 0.0)
    sb = jnp.where(is_lat, sb_ref[...], 0.0)
    for h in range(NKV_B):
        kh = k_ref[:, h * HD_B:(h + 1) * HD_B].astype(F32)
        ko_ref[0, h] = _rope(_rms(kh, kw_ref[...]), cos, sa, sb).astype(BF16)
        vo_ref[0, h] = v_ref[:, h * HD_B:(h + 1) * HD_B].astype(F32).T.astype(BF16)


def _kv_prep(p, rope, k_norm_w, batch, bpb):
    nt = p.shape[0]
    tt = bpb * TOK
    tab = pl.BlockSpec((TOK, HD_B), lambda t: (jnp.maximum(t % bpb - 1, 0), 0))
    return pl.pallas_call(
        functools.partial(_kv_prep_kernel, bpb=bpb),
        out_shape=(jax.ShapeDtypeStruct((batch, NKV_B, tt, HD_B), BF16),
                   jax.ShapeDtypeStruct((batch, NKV_B, HD_B, tt), BF16)),
        grid=(nt // TOK,),
        in_specs=[pl.BlockSpec((TOK, KV_B), lambda t: (t, P_KB // KV_B)),
                  pl.BlockSpec((TOK, KV_B), lambda t: (t, P_VB // KV_B)),
                  tab, tab, tab,
                  pl.BlockSpec((1, HD_B), lambda t: (0, 0))],
        out_specs=(pl.BlockSpec((1, NKV_B, TOK, HD_B), lambda t: (t // bpb, 0, t % bpb, 0)),
                   pl.BlockSpec((1, NKV_B, HD_B, TOK), lambda t: (t // bpb, 0, 0, t % bpb))),
        compiler_params=_params(("arbitrary",)),
        name="attn_kv_prep",
    )(p, p, *rope, k_norm_w.reshape(1, HD_B))


def _mlstm_kernel(*refs, reverse, fuse):
    if fuse:
        q_ref, k_ref, v_ref, g_ref, hb_ref, nw_ref, h_ref, c_ref, m_ref = refs
    else:
        q_ref, k_ref, v_ref, g_ref, h_ref, c_ref, m_ref = refs
    L = TOK

    @pl.when(pl.program_id(1) == 0)
    def _():
        c_ref[...] = jnp.zeros_like(c_ref)
        m_ref[...] = jnp.full_like(m_ref, M_INIT)

    rows = lax.broadcasted_iota(jnp.int32, (L, L), 0)
    cols = lax.broadcasted_iota(jnp.int32, (L, L), 1)
    mask = (cols >= rows) if reverse else (cols <= rows)
    tri = mask.astype(BF16)

    g = g_ref[...]
    lf = _log_sigmoid(g)
    lf1 = lf.astype(BF16)
    r1 = lf - lf1.astype(F32)
    lf2 = r1.astype(BF16)
    lf3 = (r1 - lf2.astype(F32)).astype(BF16)
    bsum = _dot(tri, lf1) + _dot(tri, lf2) + _dot(tri, lf3)
    bal = pltpu.roll(bsum, LANES - NH_A, axis=1)
    ib = g - bal
    ib_t = ib.T
    end = 0 if reverse else L - 1
    lane0 = 2 * NH_A if reverse else 0

    rowi = lax.broadcasted_iota(jnp.int32, (L, LANES), 0)
    cm = ib
    k = 1
    while k < L:
        if reverse:
            shifted, ok = pltpu.roll(cm, L - k, axis=0), rowi < L - k
        else:
            shifted, ok = pltpu.roll(cm, k, axis=0), rowi >= k
        cm = jnp.maximum(cm, jnp.where(ok, shifted, -jnp.inf))
        k *= 2
    m0 = m_ref[...]
    mx = jnp.maximum(cm, m0)
    a_all = jnp.exp(m0 - mx)
    m_all = bal + mx
    en_all = jnp.exp(-m_all)
    m_end = m_all[end:end + 1, :]
    a_end_all = a_all[end:end + 1, :]
    wend_all = jnp.exp(bal[end:end + 1, :] + ib - m_end)
    m_ref[...] = m_end
    ones_blk = (lax.broadcasted_iota(jnp.int32, (L, LANES), 1) == 0).astype(BF16)
    ones_dv = jnp.ones((LANES, LANES), BF16)

    for h in range(NH_A):
        li = lane0 + h
        ibrow = ib_t[li:li + 1, :]
        w = jnp.where(mask, jnp.exp(ibrow - mx[:, li:li + 1]), 0.0)
        a = a_all[:, li:li + 1]
        w_end = wend_all[:, li:li + 1]
        a_end = a_end_all[:, li:li + 1]
        qh = q_ref[:, h * DK_A:(h + 1) * DK_A]
        kh = k_ref[:, h * DK_A:(h + 1) * DK_A]
        vh = v_ref[:, h * DV_A:(h + 1) * DV_A]
        s = _dot_nt(qh, kh) * w
        st0 = c_ref[h]
        qc = _dot_nt(qh, st0.astype(BF16))
        sv = _dot(s.astype(BF16), jnp.concatenate([vh, ones_blk], axis=1))
        num = a * qc[:, 0:DV_A] + sv[:, 0:DV_A]
        den = a * qc[:, DV_A:DV_A + 1] + sv[:, DV_A:DV_A + 1]
        hh = num / jnp.maximum(jnp.abs(den), en_all[:, li:li + 1])
        if fuse:
            tot = hh + hb_ref[:, h * DV_A:(h + 1) * DV_A].astype(F32)
            sq = tot * tot
            sq = sum(sq[:, t * LANES:(t + 1) * LANES] for t in range(1, DV_A // LANES)) + sq[:, 0:LANES]
            sq_hi = sq.astype(BF16)
            sq_lo = (sq - sq_hi.astype(F32)).astype(BF16)
            ssq = _dot(sq_hi, ones_dv) + _dot(sq_lo, ones_dv)
            rs = lax.rsqrt(ssq * (1.0 / DV_A) + EPS)
            for t in range(DV_A // LANES):
                sl = slice(h * DV_A + t * LANES, h * DV_A + (t + 1) * LANES)
                h_ref[:, sl] = (tot[:, t * LANES:(t + 1) * LANES] * rs * nw_ref[:, sl]).astype(h_ref.dtype)
        else:
            h_ref[:, h * DV_A:(h + 1) * DV_A] = hh.astype(h_ref.dtype)

        vw = (vh.astype(F32) * w_end).astype(BF16)
        c_ref[h, 0:DV_A, :] = a_end * st0[0:DV_A, :] + _dot_tn(vw, kh)
        c_ref[h, DV_A:DV_A + 1, :] = (a_end * st0[DV_A:DV_A + 1, :]
                                      + jnp.sum(kh.astype(F32) * w_end, axis=0, keepdims=True))


def _mlstm_sweep(qk, p, gates, batch, bpb, reverse, h_bwd=None, norm_w=None):
    nch = bpb - 1
    fuse = h_bwd is not None

    def src(b, c):
        if reverse:
            return b * bpb + jnp.where(c == 0, 0, bpb - c)
        return b * bpb + c

    def dst(b, c):
        if reverse:
            return b * nch + nch - jnp.maximum(c, 1)
        return b * nch + jnp.maximum(c, 1) - 1

    in_specs = [pl.BlockSpec((TOK, QK_A), lambda b, c: (src(b, c), 0)),
                pl.BlockSpec((TOK, QK_A), lambda b, c: (src(b, c), 1)),
                pl.BlockSpec((TOK, V_A), lambda b, c: (src(b, c), P_VA // V_A)),
                pl.BlockSpec((TOK, LANES), lambda b, c: (src(b, c), 0))]
    args = [qk, qk, p, gates]
    if fuse:
        in_specs += [pl.BlockSpec((TOK, V_A), lambda b, c: (dst(b, c), 0)),
                     pl.BlockSpec((1, V_A), lambda b, c: (0, 0))]
        args += [h_bwd, norm_w.reshape(1, V_A)]
    return pl.pallas_call(
        functools.partial(_mlstm_kernel, reverse=reverse, fuse=fuse),
        out_shape=jax.ShapeDtypeStruct((batch * nch * TOK, V_A), BF16),
        grid=(batch, bpb),
        in_specs=in_specs,
        out_specs=pl.BlockSpec((TOK, V_A), lambda b, c: (dst(b, c), 0)),
        scratch_shapes=[pltpu.VMEM((NH_A, DV_A + LANES, DK_A), F32),
                        pltpu.VMEM((1, LANES), F32)],
        compiler_params=_params(("arbitrary", "arbitrary")),
        name="mlstm_bwd" if reverse else "mlstm_fwd",
    )(*args)


ATTN_UNROLL = 2
ATTN_KEY_CHUNKS = (1408, 384, 256, 128)


def _attn_kernel(q_ref, cos_ref, sa_ref, sb_ref, qn_ref, cosn_ref, san_ref, sbn_ref, qw_ref,
                 k_ref, kn_ref, vt_ref, o_ref, qs_ref, s_ref, m_ref, l_ref, acc_ref, *, tk):
    tq = TOK
    n_kv = k_ref.shape[2] // tk
    assert n_kv % 2 == 0

    def prep(qr_ref, c_ref, a_ref, b_ref):
        cos, sa, sb = c_ref[...], a_ref[...], b_ref[...]
        for h in range(GROUP_B):
            qh = qr_ref[:, h * HD_B:(h + 1) * HD_B].astype(F32)
            qr = _rope(_rms(qh, qw_ref[...]), cos, sa, sb)
            qs_ref[h * tq:(h + 1) * tq, :] = (qr * (HD_B ** -0.5 * LOG2_E)).astype(BF16)

    def scores(c, slot):
        off = pl.multiple_of(c * tk, tk)
        s_ref[slot] = _dot_nt(k_ref[0, 0, pl.ds(off, tk), :], qs_ref[...])

    def consume(c, slot):
        off = pl.multiple_of(c * tk, LANES)
        s = s_ref[slot]
        m_old = m_ref[...]
        m_new = jnp.maximum(m_old, jnp.max(s, axis=0, keepdims=True))
        pexp = jnp.exp2(s - m_new)
        alpha = jnp.exp2(m_old - m_new)
        l_ref[...] = alpha * l_ref[...] + jnp.sum(pexp, axis=0, keepdims=True)
        acc_ref[...] = alpha * acc_ref[...] + _dot(vt_ref[0, 0, :, pl.ds(off, tk)], pexp.astype(BF16))
        m_ref[...] = m_new

    first = jnp.logical_and(pl.program_id(0) == 0,
                            jnp.logical_and(pl.program_id(1) == 0, pl.program_id(2) == 0))

    @pl.when(first)
    def _():
        prep(q_ref, cos_ref, sa_ref, sb_ref)
        scores(0, 0)

    m_ref[...] = jnp.full_like(m_ref, -jnp.inf)
    l_ref[...] = jnp.zeros_like(l_ref)
    acc_ref[...] = jnp.zeros_like(acc_ref)

    def trip(t, carry):
        for u in range(ATTN_UNROLL):
            c = ATTN_UNROLL * t + u
            scores(c + 1, (u + 1) % 2)
            consume(c, u % 2)
        return carry

    n_trips = (n_kv - 2) // ATTN_UNROLL
    lax.fori_loop(0, n_trips, trip, 0)
    for c in range(ATTN_UNROLL * n_trips, n_kv - 1):
        scores(c + 1, (c + 1) % 2)
        consume(c, c % 2)
    prep(qn_ref, cosn_ref, san_ref, sbn_ref)
    s_ref[0] = _dot_nt(kn_ref[0, 0], qs_ref[...])
    consume(n_kv - 1, 1)

    o_t = acc_ref[...] / l_ref[...]
    for h in range(GROUP_B):
        o_ref[:, h * HD_B:(h + 1) * HD_B] = o_t[:, h * tq:(h + 1) * tq].T.astype(BF16)


def _attention(p, k_all, v_all, rope, q_norm_w, batch, bpb):
    nch = bpb - 1
    tt = bpb * TOK
    tk = next(c for c in ATTN_KEY_CHUNKS if tt % c == 0 and (tt // c) % 2 == 0)
    gw = GROUP_B * HD_B
    nq = GROUP_B * TOK

    def nxt(b, g, i):
        i2 = (i + 1) % nch
        g1 = g + (i + 1) // nch
        return jnp.minimum(b + g1 // NKV_B, batch - 1), g1 % NKV_B, i2

    def q_spec(f):
        return pl.BlockSpec((TOK, gw), lambda b, g, i: (f(b, g, i)[0] * bpb + 1 + f(b, g, i)[2],
                                                        P_QB // gw + f(b, g, i)[1]))

    def tab_spec(f):
        return pl.BlockSpec((TOK, HD_B), lambda b, g, i: (f(b, g, i)[2], 0))

    cur = lambda b, g, i: (b, g, i)
    return pl.pallas_call(
        functools.partial(_attn_kernel, tk=tk),
        out_shape=jax.ShapeDtypeStruct((batch * nch * TOK, Q_B), BF16),
        grid=(batch, NKV_B, nch),
        in_specs=[q_spec(cur), tab_spec(cur), tab_spec(cur), tab_spec(cur),
                  q_spec(nxt), tab_spec(nxt), tab_spec(nxt), tab_spec(nxt),
                  pl.BlockSpec((1, HD_B), lambda b, g, i: (0, 0)),
                  pl.BlockSpec((1, 1, tt, HD_B), lambda b, g, i: (b, g, 0, 0)),
                  pl.BlockSpec((1, 1, tk, HD_B), lambda b, g, i: (nxt(b, g, i)[0], nxt(b, g, i)[1], 0, 0)),
                  pl.BlockSpec((1, 1, HD_B, tt), lambda b, g, i: (b, g, 0, 0))],
        out_specs=pl.BlockSpec((TOK, gw), lambda b, g, i: (b * nch + i, g)),
        scratch_shapes=[pltpu.VMEM((nq, HD_B), BF16),
                        pltpu.VMEM((2, tk, nq), F32),
                        pltpu.VMEM((1, nq), F32),
                        pltpu.VMEM((1, nq), F32),
                        pltpu.VMEM((HD_B, nq), F32)],
        compiler_params=_params(("arbitrary", "arbitrary", "arbitrary")),
        name="attention",
    )(p, *rope, p, *rope, q_norm_w.reshape(1, HD_B), k_all, k_all, v_all)


def _branch_kernel(hn_ref, oa_ref, za_ref, at_ref, zb_ref, ga_ref, gb_ref, wa_ref, wb_ref, y_ref):
    a = _sigmoid(oa_ref[...].astype(F32)) * hn_ref[...].astype(F32) * _silu(za_ref[...].astype(F32))
    ya = _dot(a.astype(BF16), wa_ref[...])
    b = at_ref[...].astype(F32) * _silu(zb_ref[...].astype(F32))
    yb = _dot(b.astype(BF16), wb_ref[...])
    y = _sigmoid(ga_ref[...].astype(F32)) * ya + _sigmoid(gb_ref[...].astype(F32)) * yb
    y_ref[...] = y.astype(BF16)


def _branches(hn, o_attn, p, w_ba, w_bb, bpb):
    nl = hn.shape[0]
    nch = bpb - 1
    tm = TOK

    def col(k):
        return pl.BlockSpec((tm, D_MODEL), lambda i: ((i // nch) * bpb + 1 + i % nch, k // D_MODEL))

    own = pl.BlockSpec((tm, D_MODEL), lambda i: (i, 0))
    resident = pl.BlockSpec(memory_space=pltpu.VMEM)
    return pl.pallas_call(
        _branch_kernel,
        out_shape=jax.ShapeDtypeStruct((nl, D_MODEL), BF16),
        grid=(nl // tm,),
        in_specs=[own, col(P_OA), col(P_ZA), own, col(P_ZB), col(P_GA), col(P_GB), resident, resident],
        out_specs=own,
        compiler_params=_params(("arbitrary",)),
        name="branch_merge",
    )(hn, p, p, o_attn, p, p, p, w_ba, w_bb)


def _out_kernel(y_ref, x_ref, mod_ref, w_ref, lw_ref, lb_ref, o_ref, *, tiles_per_sample):
    b = pl.program_id(0) // tiles_per_sample
    gate = mod_ref[pl.ds(b, 1), 2 * D_MODEL:3 * D_MODEL]
    z = ALPHA * x_ref[...] + gate * _dot(y_ref[...], w_ref[...])
    mu = jnp.mean(z, axis=-1, keepdims=True)
    zc = z - mu
    var = jnp.mean(zc * zc, axis=-1, keepdims=True)
    o_ref[...] = zc * lax.rsqrt(var + EPS) * lw_ref[...] + lb_ref[...]


def _out_proj(y, x2, mod, w_out, ln_w, ln_b, seq):
    nl = y.shape[0]
    tm = _pick(seq, (512, 256))
    tile = pl.BlockSpec((tm, D_MODEL), lambda i: (i, 0))
    vec = pl.BlockSpec((1, D_MODEL), lambda i: (0, 0))
    return pl.pallas_call(
        functools.partial(_out_kernel, tiles_per_sample=seq // tm),
        out_shape=jax.ShapeDtypeStruct((nl, D_MODEL), F32),
        grid=(nl // tm,),
        in_specs=[tile, tile, pl.BlockSpec(mod.shape, lambda i: (0, 0)),
                  pl.BlockSpec(memory_space=pltpu.VMEM), vec, vec],
        out_specs=tile,
        compiler_params=_params(("arbitrary",)),
        name="out_proj",
    )(y, x2, mod, w_out, ln_w.reshape(1, -1), ln_b.reshape(1, -1))


def _rope_tables(seq):
    t = jnp.arange(seq)
    inv = ROPE_THETA ** (-jnp.arange(0, ROT_HALF, 2, dtype=F32) / ROT_HALF)
    ang_r = (t // GRID_W).astype(F32)[:, None] * inv[None]
    ang_c = (t % GRID_W).astype(F32)[:, None] * inv[None]
    cr, sr, cc, sc = jnp.cos(ang_r), jnp.sin(ang_r), jnp.cos(ang_c), jnp.sin(ang_c)
    z = jnp.zeros_like(sr)
    cos = jnp.concatenate([cr, cr, cc, cc], axis=1)
    sin_a = jnp.concatenate([-sr, z, -sc, z], axis=1)
    sin_b = jnp.concatenate([z, sr, z, sc], axis=1)
    return cos, sin_a, sin_b


def _block_out(x, c, ctx, c_ctx, w_mod, b_mod, w_in, b_if, conv_w, conv_b, mh_norm_w,
               q_norm_w, k_norm_w, w_ba, w_bb):
    batch, seq, d = x.shape
    assert d == D_MODEL and ctx.shape == (batch, TOK, D_MODEL) and seq % TOK == 0 and seq % GRID_W == 0
    assert w_in.shape == (D_MODEL, N_IN)
    bpb = seq // TOK + 1

    rows = -(-(batch + 1) // 8) * 8
    cc = jnp.zeros((rows, D_MODEL), F32).at[:batch].set(c).at[batch].set(c_ctx)
    mod = _modulation(cc, w_mod, b_mod)

    xc = jnp.concatenate([ctx, x], axis=1).reshape(batch * bpb * TOK, D_MODEL)
    w_p = jnp.concatenate([w_in[:, :W_IF], w_in[:, W_OA:], w_in[:, W_KB:W_OA]], axis=1).astype(BF16)
    w_if = jnp.pad(w_in[:, W_IF:W_KB], ((0, 0), (0, LANES - N_GATES))).astype(BF16)
    b_if_p = jnp.pad(b_if, (0, LANES - N_GATES)).reshape(1, LANES)
    p, gates = _in_proj(xc, mod, w_p, w_if, b_if_p, bpb, batch)

    rope = _rope_tables(seq)
    qk = _qk_conv(p, conv_w, conv_b, bpb)
    k_all, v_all = _kv_prep(p, rope, k_norm_w, batch, bpb)
    h_bwd = _mlstm_sweep(qk, p, gates, batch, bpb, reverse=True)
    hn = _mlstm_sweep(qk, p, gates, batch, bpb, reverse=False, h_bwd=h_bwd, norm_w=mh_norm_w)
    o_attn = _attention(p, k_all, v_all, rope, q_norm_w, batch, bpb)
    y = _branches(hn, o_attn, p, w_ba.astype(BF16), w_bb.astype(BF16), bpb)
    return y, mod


def kernel(x, c, ctx, c_ctx, w_mod, b_mod, w_in, b_if, conv_w, conv_b, mh_norm_w, q_norm_w, k_norm_w,
           w_branch_a, w_branch_b, w_out, ln_w, ln_b):
    assert w_mod.shape[0] == DEPTH
    batch, seq, _ = x.shape
    y, mod = _block_out(x, c, ctx, c_ctx, w_mod[0], b_mod[0], w_in[0], b_if[0], conv_w[0], conv_b[0],
                        mh_norm_w[0], q_norm_w[0], k_norm_w[0], w_branch_a[0], w_branch_b[0])
    out = _out_proj(y, x.reshape(batch * seq, D_MODEL), mod, w_out[0].astype(BF16), ln_w[0], ln_b[0], seq)
    return out.reshape(batch, seq, D_MODEL)
```

```python
import functools

import jax
import jax.numpy as jnp
from jax import lax
from jax.experimental import pallas as pl
from jax.experimental.pallas import tpu as pltpu

F32 = jnp.float32
BF16 = jnp.bfloat16

D_MODEL = 2048
GRID_W = 64
NH_A, DK_A, DV_A = 8, 128, 256
QK_A, V_A = NH_A * DK_A, NH_A * DV_A
N_GATES = 4 * NH_A
NH_B, NKV_B, HD_B = 16, 4, 128
GROUP_B = NH_B // NKV_B
Q_B, KV_B = NH_B * HD_B, NKV_B * HD_B
ROT_HALF = HD_B // 2
ROPE_THETA = 10000.0
EPS = 1e-6
DEPTH = 1
ALPHA = (2 * DEPTH) ** 0.25
M_INIT = -1e30
LOG2_E = 1.4426950408889634

LANES = 128
BF16_SUBLANES = 16
VMEM_LIMIT = 60 * 1024 * 1024

TOK = 256

P_QK, P_VA, P_OA, P_ZA, P_QB, P_ZB, P_GA, P_GB = (i * D_MODEL for i in range(8))
P_KB = 8 * D_MODEL
P_VB = P_KB + KV_B
N_P = P_VB + KV_B
W_IF = 2 * QK_A + V_A
W_KB = W_IF + N_GATES
W_OA = W_KB + 2 * KV_B
N_IN = W_OA + 2 * V_A + 2 * Q_B + 2 * D_MODEL


def _sigmoid(x):
    return 1.0 / (1.0 + jnp.exp(-x))


def _silu(x):
    return x * _sigmoid(x)


def _log_sigmoid(x):
    return jnp.minimum(x, 0.0) - jnp.log(1.0 + jnp.exp(-jnp.abs(x)))


def _dot(a, b):
    return jnp.dot(a, b, preferred_element_type=F32)


def _dot_nt(a, b):
    return lax.dot_general(a, b, (((1,), (1,)), ((), ())), preferred_element_type=F32)


def _dot_tn(a, b):
    return lax.dot_general(a, b, (((0,), (0,)), ((), ())), preferred_element_type=F32)


def _params(semantics):
    return pltpu.CompilerParams(dimension_semantics=semantics, vmem_limit_bytes=VMEM_LIMIT)


def _pick(n, candidates):
    for c in candidates:
        if n % c == 0:
            return c
    raise ValueError(f"no tile in {candidates} divides {n}")


def _mod_kernel(c_ref, w_ref, b_ref, o_ref):
    a = _silu(c_ref[...]).astype(BF16)
    o_ref[...] = _dot(a, w_ref[...].astype(BF16)) + b_ref[...]


def _modulation(cc, w_mod, b_mod):
    rows = cc.shape[0]
    n = w_mod.shape[1]
    tn = _pick(n, (768, 512, 256, 128))
    return pl.pallas_call(
        _mod_kernel,
        out_shape=jax.ShapeDtypeStruct((rows, n), F32),
        grid=(n // tn,),
        in_specs=[pl.BlockSpec((rows, D_MODEL), lambda j: (0, 0)),
                  pl.BlockSpec((D_MODEL, tn), lambda j: (0, j)),
                  pl.BlockSpec((1, tn), lambda j: (0, j))],
        out_specs=pl.BlockSpec((rows, tn), lambda j: (0, j)),
        compiler_params=_params(("arbitrary",)),
        name="modulation",
    )(cc, w_mod, b_mod.reshape(1, n))


def _in_proj_kernel(*refs, tm, bpb, ctx_row, bounds):
    x_ref, mod_ref = refs[:2]
    w_refs = refs[2:len(bounds) + 1]
    wif_ref, bif_ref, p_ref, g_ref, u_ref = refs[len(bounds) + 1:]
    i = pl.program_id(0)
    j = pl.program_id(1)

    @pl.when(j == 0)
    def _():
        for s in range(tm // TOK):
            t = i * (tm // TOK) + s
            row = jnp.where(t % bpb == 0, ctx_row, t // bpb)
            xs = x_ref[s * TOK:(s + 1) * TOK, :]
            mu = jnp.mean(xs, axis=-1, keepdims=True)
            xc = xs - mu
            var = jnp.mean(xc * xc, axis=-1, keepdims=True)
            y = xc * lax.rsqrt(var + EPS)
            shift = mod_ref[pl.ds(row, 1), 0:D_MODEL]
            scale = mod_ref[pl.ds(row, 1), D_MODEL:2 * D_MODEL]
            u_ref[s * TOK:(s + 1) * TOK, :] = (y * (1.0 + scale) + shift).astype(BF16)
        g_ref[...] = _dot(u_ref[...], wif_ref[...]) + bif_ref[...]

    for w_ref, lo, hi in zip(w_refs, bounds[:-1], bounds[1:]):
        @pl.when(jnp.logical_and(j >= lo, j < hi))
        def _(w_ref=w_ref):
            p_ref[...] = _dot(u_ref[...], w_ref[...]).astype(BF16)


def _in_proj(xc, mod, w_groups, w_if, b_if, bpb, ctx_row):
    nt = xc.shape[0]
    tm = _pick(nt, (1024, 512, 256))
    tn = 1024
    counts = [w.shape[1] // tn for w in w_groups]
    assert all(w.shape[1] % tn == 0 for w in w_groups) and sum(counts) == N_P // tn
    bounds = [sum(counts[:k]) for k in range(len(counts) + 1)]
    kern = functools.partial(_in_proj_kernel, tm=tm, bpb=bpb, ctx_row=ctx_row, bounds=bounds)

    def w_spec(lo, n):
        return pl.BlockSpec((D_MODEL, tn), lambda i, j: (0, jnp.clip(j - lo, 0, n - 1)))
    return pl.pallas_call(
        kern,
        out_shape=(jax.ShapeDtypeStruct((nt, N_P), BF16), jax.ShapeDtypeStruct((nt, LANES), F32)),
        grid=(nt // tm, N_P // tn),
        in_specs=[pl.BlockSpec((tm, D_MODEL), lambda i, j: (i, 0)),
                  pl.BlockSpec(mod.shape, lambda i, j: (0, 0)),
                  *[w_spec(lo, n) for lo, n in zip(bounds[:-1], counts)],
                  pl.BlockSpec((D_MODEL, LANES), lambda i, j: (0, 0)),
                  pl.BlockSpec((1, LANES), lambda i, j: (0, 0))],
        out_specs=(pl.BlockSpec((tm, tn), lambda i, j: (i, j)),
                   pl.BlockSpec((tm, LANES), lambda i, j: (i, 0))),
        scratch_shapes=[pltpu.VMEM((tm, D_MODEL), BF16)],
        compiler_params=_params(("arbitrary", "arbitrary")),
        name="in_proj",
    )(xc, mod, *w_groups, w_if, b_if)


def _qk_conv_kernel(x_ref, prev_ref, next_ref, cw_ref, cb_ref, o_ref, *, bpb):
    tb = pl.program_id(0) % bpb
    has_prev = (tb >= 2).astype(F32)
    has_next = jnp.logical_and(tb >= 1, tb < bpb - 1).astype(F32)
    x = x_ref[...].astype(F32)
    prev_row = prev_ref[BF16_SUBLANES - 1:BF16_SUBLANES, :].astype(F32) * has_prev
    next_row = next_ref[0:1, :].astype(F32) * has_next
    rows = lax.broadcasted_iota(jnp.int32, (TOK, 1), 0)
    x_prev = jnp.where(rows == 0, prev_row, pltpu.roll(x, 1, axis=0))
    x_next = jnp.where(rows == TOK - 1, next_row, pltpu.roll(x, TOK - 1, axis=0))
    y = cb_ref[...] + x_prev * cw_ref[0:1, :] + x * cw_ref[1:2, :] + x_next * cw_ref[2:3, :]
    qk = _silu(y)
    o_ref[:, 0:QK_A] = qk[:, 0:QK_A].astype(BF16)
    o_ref[:, QK_A:2 * QK_A] = (qk[:, QK_A:2 * QK_A] * DK_A ** -0.5).astype(BF16)


def _qk_conv(p, conv_w, conv_b, bpb):
    nt = p.shape[0]
    nblk = nt // TOK
    sub = TOK // BF16_SUBLANES
    last = nt // BF16_SUBLANES - 1
    return pl.pallas_call(
        functools.partial(_qk_conv_kernel, bpb=bpb),
        out_shape=jax.ShapeDtypeStruct((nt, 2 * QK_A), BF16),
        grid=(nblk,),
        in_specs=[pl.BlockSpec((TOK, 2 * QK_A), lambda t: (t, P_QK // (2 * QK_A))),
                  pl.BlockSpec((BF16_SUBLANES, 2 * QK_A), lambda t: (jnp.maximum(t * sub - 1, 0), 0)),
                  pl.BlockSpec((BF16_SUBLANES, 2 * QK_A), lambda t: (jnp.minimum((t + 1) * sub, last), 0)),
                  pl.BlockSpec((3, 2 * QK_A), lambda t: (0, 0)),
                  pl.BlockSpec((1, 2 * QK_A), lambda t: (0, 0))],
        out_specs=pl.BlockSpec((TOK, 2 * QK_A), lambda t: (t, 0)),
        compiler_params=_params(("arbitrary",)),
        name="mlstm_qk_conv",
    )(p, p, p, conv_w, conv_b.reshape(1, -1))


def _rope(x, cos, sin_a, sin_b):
    half = ROT_HALF // 2
    return x * cos + pltpu.roll(x, HD_B - half, axis=1) * sin_a + pltpu.roll(x, half, axis=1) * sin_b


def _rms(x, w):
    return x * lax.rsqrt(jnp.mean(x * x, axis=-1, keepdims=True) + EPS) * w


def _kv_prep_kernel(k_ref, v_ref, cos_ref, sa_ref, sb_ref, kw_ref, ko_ref, vo_ref, *, bpb):
    is_lat = pl.program_id(0) % bpb > 0
    cos = jnp.where(is_lat, cos_ref[...], 1.0)
    sa = jnp.where(is_lat, sa_ref[...], 0.0)
    sb = jnp.where(is_lat, sb_ref[...], 0.0)
    for h in range(NKV_B):
        kh = k_ref[:, h * HD_B:(h + 1) * HD_B].astype(F32)
        ko_ref[0, h] = _rope(_rms(kh, kw_ref[...]), cos, sa, sb).astype(BF16)
        vo_ref[0, h] = v_ref[:, h * HD_B:(h + 1) * HD_B].astype(F32).T.astype(BF16)


def _kv_prep(p, rope, k_norm_w, batch, bpb):
    nt = p.shape[0]
    tt = bpb * TOK
    tab = pl.BlockSpec((TOK, HD_B), lambda t: (jnp.maximum(t % bpb - 1, 0), 0))
    return pl.pallas_call(
        functools.partial(_kv_prep_kernel, bpb=bpb),
        out_shape=(jax.ShapeDtypeStruct((batch, NKV_B, tt, HD_B), BF16),
                   jax.ShapeDtypeStruct((batch, NKV_B, HD_B, tt), BF16)),
        grid=(nt // TOK,),
        in_specs=[pl.BlockSpec((TOK, KV_B), lambda t: (t, P_KB // KV_B)),
                  pl.BlockSpec((TOK, KV_B), lambda t: (t, P_VB // KV_B)),
                  tab, tab, tab,
                  pl.BlockSpec((1, HD_B), lambda t: (0, 0))],
        out_specs=(pl.BlockSpec((1, NKV_B, TOK, HD_B), lambda t: (t // bpb, 0, t % bpb, 0)),
                   pl.BlockSpec((1, NKV_B, HD_B, TOK), lambda t: (t // bpb, 0, 0, t % bpb))),
        compiler_params=_params(("arbitrary",)),
        name="attn_kv_prep",
    )(p, p, *rope, k_norm_w.reshape(1, HD_B))


def _mlstm_kernel(*refs, reverse, fuse):
    if fuse:
        q_ref, k_ref, v_ref, g_ref, hb_ref, nw_ref, h_ref, c_ref, m_ref = refs
    else:
        q_ref, k_ref, v_ref, g_ref, h_ref, c_ref, m_ref = refs
    L = TOK

    @pl.when(pl.program_id(1) == 0)
    def _():
        c_ref[...] = jnp.zeros_like(c_ref)
        m_ref[...] = jnp.full_like(m_ref, M_INIT)

    rows = lax.broadcasted_iota(jnp.int32, (L, L), 0)
    cols = lax.broadcasted_iota(jnp.int32, (L, L), 1)
    mask = (cols >= rows) if reverse else (cols <= rows)
    tri = mask.astype(BF16)

    g = g_ref[...]
    lf = _log_sigmoid(g)
    lf1 = lf.astype(BF16)
    r1 = lf - lf1.astype(F32)
    lf2 = r1.astype(BF16)
    lf3 = (r1 - lf2.astype(F32)).astype(BF16)
    bsum = _dot(tri, lf1) + _dot(tri, lf2) + _dot(tri, lf3)
    bal = pltpu.roll(bsum, LANES - NH_A, axis=1)
    ib = g - bal
    ib_t = ib.T
    end = 0 if reverse else L - 1
    lane0 = 2 * NH_A if reverse else 0

    rowi = lax.broadcasted_iota(jnp.int32, (L, LANES), 0)
    cm = ib
    k = 1
    while k < L:
        if reverse:
            shifted, ok = pltpu.roll(cm, L - k, axis=0), rowi < L - k
        else:
            shifted, ok = pltpu.roll(cm, k, axis=0), rowi >= k
        cm = jnp.maximum(cm, jnp.where(ok, shifted, -jnp.inf))
        k *= 2
    m0 = m_ref[...]
    mx = jnp.maximum(cm, m0)
    a_all = jnp.exp(m0 - mx)
    m_all = bal + mx
    en_all = jnp.exp(-m_all)
    m_end = m_all[end:end + 1, :]
    a_end_all = a_all[end:end + 1, :]
    wend_all = jnp.exp(bal[end:end + 1, :] + ib - m_end)
    m_ref[...] = m_end
    ones_blk = (lax.broadcasted_iota(jnp.int32, (L, LANES), 1) == 0).astype(BF16)
    ones_dv = jnp.ones((LANES, LANES), BF16)

    for h in range(NH_A):
        li = lane0 + h
        ibrow = ib_t[li:li + 1, :]
        w = jnp.where(mask, jnp.exp(ibrow - mx[:, li:li + 1]), 0.0)
        a = a_all[:, li:li + 1]
        w_end = wend_all[:, li:li + 1]
        a_end = a_end_all[:, li:li + 1]
        qh = q_ref[:, h * DK_A:(h + 1) * DK_A]
        kh = k_ref[:, h * DK_A:(h + 1) * DK_A]
        vh = v_ref[:, h * DV_A:(h + 1) * DV_A]
        s = _dot_nt(qh, kh) * w
        st0 = c_ref[h]
        qc = _dot_nt(qh, st0.astype(BF16))
        sv = _dot(s.astype(BF16), jnp.concatenate([vh, ones_blk], axis=1))
        num = a * qc[:, 0:DV_A] + sv[:, 0:DV_A]
        den = a * qc[:, DV_A:DV_A + 1] + sv[:, DV_A:DV_A + 1]
        hh = num / jnp.maximum(jnp.abs(den), en_all[:, li:li + 1])
        if fuse:
            tot = hh + hb_ref[:, h * DV_A:(h + 1) * DV_A].astype(F32)
            sq = tot * tot
            sq = sum(sq[:, t * LANES:(t + 1) * LANES] for t in range(1, DV_A // LANES)) + sq[:, 0:LANES]
            sq_hi = sq.astype(BF16)
            sq_lo = (sq - sq_hi.astype(F32)).astype(BF16)
            ssq = _dot(sq_hi, ones_dv) + _dot(sq_lo, ones_dv)
            rs = lax.rsqrt(ssq * (1.0 / DV_A) + EPS)
            for t in range(DV_A // LANES):
                sl = slice(h * DV_A + t * LANES, h * DV_A + (t + 1) * LANES)
                h_ref[:, sl] = (tot[:, t * LANES:(t + 1) * LANES] * rs * nw_ref[:, sl]).astype(h_ref.dtype)
        else:
            h_ref[:, h * DV_A:(h + 1) * DV_A] = hh.astype(h_ref.dtype)

        vw = (vh.astype(F32) * w_end).astype(BF16)
        c_ref[h, 0:DV_A, :] = a_end * st0[0:DV_A, :] + _dot_tn(vw, kh)
        c_ref[h, DV_A:DV_A + 1, :] = (a_end * st0[DV_A:DV_A + 1, :]
                                      + jnp.sum(kh.astype(F32) * w_end, axis=0, keepdims=True))


def _mlstm_sweep(qk, p, gates, batch, bpb, reverse, h_bwd=None, norm_w=None):
    nch = bpb - 1
    fuse = h_bwd is not None

    def src(b, c):
        if reverse:
            return b * bpb + jnp.where(c == 0, 0, bpb - c)
        return b * bpb + c

    def dst(b, c):
        if reverse:
            return b * nch + nch - jnp.maximum(c, 1)
        return b * nch + jnp.maximum(c, 1) - 1

    in_specs = [pl.BlockSpec((TOK, QK_A), lambda b, c: (src(b, c), 0)),
                pl.BlockSpec((TOK, QK_A), lambda b, c: (src(b, c), 1)),
                pl.BlockSpec((TOK, V_A), lambda b, c: (src(b, c), P_VA // V_A)),
                pl.BlockSpec((TOK, LANES), lambda b, c: (src(b, c), 0))]
    args = [qk, qk, p, gates]
    if fuse:
        in_specs += [pl.BlockSpec((TOK, V_A), lambda b, c: (dst(b, c), 0)),
                     pl.BlockSpec((1, V_A), lambda b, c: (0, 0))]
        args += [h_bwd, norm_w.reshape(1, V_A)]
    return pl.pallas_call(
        functools.partial(_mlstm_kernel, reverse=reverse, fuse=fuse),
        out_shape=jax.ShapeDtypeStruct((batch * nch * TOK, V_A), BF16),
        grid=(batch, bpb),
        in_specs=in_specs,
        out_specs=pl.BlockSpec((TOK, V_A), lambda b, c: (dst(b, c), 0)),
        scratch_shapes=[pltpu.VMEM((NH_A, DV_A + LANES, DK_A), F32),
                        pltpu.VMEM((1, LANES), F32)],
        compiler_params=_params(("arbitrary", "arbitrary")),
        name="mlstm_bwd" if reverse else "mlstm_fwd",
    )(*args)


ATTN_UNROLL = 2
ATTN_KEY_CHUNKS = (1408, 384, 256, 128)


def _attn_kernel(q_ref, cos_ref, sa_ref, sb_ref, qn_ref, cosn_ref, san_ref, sbn_ref, qw_ref,
                 k_ref, kn_ref, vt_ref, o_ref, qs_ref, s_ref, m_ref, l_ref, acc_ref, *, tk):
    tq = TOK
    n_kv = k_ref.shape[2] // tk
    assert n_kv % 2 == 0

    def prep(qr_ref, c_ref, a_ref, b_ref):
        cos, sa, sb = c_ref[...], a_ref[...], b_ref[...]
        for h in range(GROUP_B):
            qh = qr_ref[:, h * HD_B:(h + 1) * HD_B].astype(F32)
            qr = _rope(_rms(qh, qw_ref[...]), cos, sa, sb)
            qs_ref[h * tq:(h + 1) * tq, :] = (qr * (HD_B ** -0.5 * LOG2_E)).astype(BF16)

    def scores(c, slot):
        off = pl.multiple_of(c * tk, tk)
        s_ref[slot] = _dot_nt(k_ref[0, 0, pl.ds(off, tk), :], qs_ref[...])

    def consume(c, slot):
        off = pl.multiple_of(c * tk, LANES)
        s = s_ref[slot]
        m_old = m_ref[...]
        m_new = jnp.maximum(m_old, jnp.max(s, axis=0, keepdims=True))
        pexp = jnp.exp2(s - m_new)
        alpha = jnp.exp2(m_old - m_new)
        l_ref[...] = alpha * l_ref[...] + jnp.sum(pexp, axis=0, keepdims=True)
        acc_ref[...] = alpha * acc_ref[...] + _dot(vt_ref[0, 0, :, pl.ds(off, tk)], pexp.astype(BF16))
        m_ref[...] = m_new

    first = jnp.logical_and(pl.program_id(0) == 0,
                            jnp.logical_and(pl.program_id(1) == 0, pl.program_id(2) == 0))

    @pl.when(first)
    def _():
        prep(q_ref, cos_ref, sa_ref, sb_ref)
        scores(0, 0)

    m_ref[...] = jnp.full_like(m_ref, -jnp.inf)
    l_ref[...] = jnp.zeros_like(l_ref)
    acc_ref[...] = jnp.zeros_like(acc_ref)

    def trip(t, carry):
        for u in range(ATTN_UNROLL):
            c = ATTN_UNROLL * t + u
            scores(c + 1, (u + 1) % 2)
            consume(c, u % 2)
        return carry

    n_trips = (n_kv - 2) // ATTN_UNROLL
    lax.fori_loop(0, n_trips, trip, 0)
    for c in range(ATTN_UNROLL * n_trips, n_kv - 1):
        scores(c + 1, (c + 1) % 2)
        consume(c, c % 2)
    prep(qn_ref, cosn_ref, san_ref, sbn_ref)
    s_ref[0] = _dot_nt(kn_ref[0, 0], qs_ref[...])
    consume(n_kv - 1, 1)

    o_t = acc_ref[...] / l_ref[...]
    for h in range(GROUP_B):
        o_ref[:, h * HD_B:(h + 1) * HD_B] = o_t[:, h * tq:(h + 1) * tq].T.astype(BF16)


def _attention(p, k_all, v_all, rope, q_norm_w, batch, bpb):
    nch = bpb - 1
    tt = bpb * TOK
    tk = next(c for c in ATTN_KEY_CHUNKS if tt % c == 0 and (tt // c) % 2 == 0)
    gw = GROUP_B * HD_B
    nq = GROUP_B * TOK

    def nxt(b, g, i):
        i2 = (i + 1) % nch
        g1 = g + (i + 1) // nch
        return jnp.minimum(b + g1 // NKV_B, batch - 1), g1 % NKV_B, i2

    def q_spec(f):
        return pl.BlockSpec((TOK, gw), lambda b, g, i: (f(b, g, i)[0] * bpb + 1 + f(b, g, i)[2],
                                                        P_QB // gw + f(b, g, i)[1]))

    def tab_spec(f):
        return pl.BlockSpec((TOK, HD_B), lambda b, g, i: (f(b, g, i)[2], 0))

    cur = lambda b, g, i: (b, g, i)
    return pl.pallas_call(
        functools.partial(_attn_kernel, tk=tk),
        out_shape=jax.ShapeDtypeStruct((batch * nch * TOK, Q_B), BF16),
        grid=(batch, NKV_B, nch),
        in_specs=[q_spec(cur), tab_spec(cur), tab_spec(cur), tab_spec(cur),
                  q_spec(nxt), tab_spec(nxt), tab_spec(nxt), tab_spec(nxt),
                  pl.BlockSpec((1, HD_B), lambda b, g, i: (0, 0)),
                  pl.BlockSpec((1, 1, tt, HD_B), lambda b, g, i: (b, g, 0, 0)),
                  pl.BlockSpec((1, 1, tk, HD_B), lambda b, g, i: (nxt(b, g, i)[0], nxt(b, g, i)[1], 0, 0)),
                  pl.BlockSpec((1, 1, HD_B, tt), lambda b, g, i: (b, g, 0, 0))],
        out_specs=pl.BlockSpec((TOK, gw), lambda b, g, i: (b * nch + i, g)),
        scratch_shapes=[pltpu.VMEM((nq, HD_B), BF16),
                        pltpu.VMEM((2, tk, nq), F32),
                        pltpu.VMEM((1, nq), F32),
                        pltpu.VMEM((1, nq), F32),
                        pltpu.VMEM((HD_B, nq), F32)],
        compiler_params=_params(("arbitrary", "arbitrary", "arbitrary")),
        name="attention",
    )(p, *rope, p, *rope, q_norm_w.reshape(1, HD_B), k_all, k_all, v_all)


def _branch_kernel(hn_ref, oa_ref, za_ref, at_ref, zb_ref, ga_ref, gb_ref, wa_ref, wb_ref, y_ref):
    a = _sigmoid(oa_ref[...].astype(F32)) * hn_ref[...].astype(F32) * _silu(za_ref[...].astype(F32))
    ya = _dot(a.astype(BF16), wa_ref[...])
    b = at_ref[...].astype(F32) * _silu(zb_ref[...].astype(F32))
    yb = _dot(b.astype(BF16), wb_ref[...])
    y = _sigmoid(ga_ref[...].astype(F32)) * ya + _sigmoid(gb_ref[...].astype(F32)) * yb
    y_ref[...] = y.astype(BF16)


def _branches(hn, o_attn, p, w_ba, w_bb, bpb):
    nl = hn.shape[0]
    nch = bpb - 1
    tm = TOK

    def col(k):
        return pl.BlockSpec((tm, D_MODEL), lambda i: ((i // nch) * bpb + 1 + i % nch, k // D_MODEL))

    own = pl.BlockSpec((tm, D_MODEL), lambda i: (i, 0))
    resident = pl.BlockSpec(memory_space=pltpu.VMEM)
    return pl.pallas_call(
        _branch_kernel,
        out_shape=jax.ShapeDtypeStruct((nl, D_MODEL), BF16),
        grid=(nl // tm,),
        in_specs=[own, col(P_OA), col(P_ZA), own, col(P_ZB), col(P_GA), col(P_GB), resident, resident],
        out_specs=own,
        compiler_params=_params(("arbitrary",)),
        name="branch_merge",
    )(hn, p, p, o_attn, p, p, p, w_ba, w_bb)


def _out_kernel(y_ref, x_ref, mod_ref, w_ref, lw_ref, lb_ref, o_ref, *, tiles_per_sample):
    b = pl.program_id(0) // tiles_per_sample
    gate = mod_ref[pl.ds(b, 1), 2 * D_MODEL:3 * D_MODEL]
    z = ALPHA * x_ref[...] + gate * _dot(y_ref[...], w_ref[...])
    mu = jnp.mean(z, axis=-1, keepdims=True)
    zc = z - mu
    var = jnp.mean(zc * zc, axis=-1, keepdims=True)
    o_ref[...] = zc * lax.rsqrt(var + EPS) * lw_ref[...] + lb_ref[...]


def _out_proj(y, x2, mod, w_out, ln_w, ln_b, seq):
    nl = y.shape[0]
    tm = _pick(seq, (512, 256))
    tile = pl.BlockSpec((tm, D_MODEL), lambda i: (i, 0))
    vec = pl.BlockSpec((1, D_MODEL), lambda i: (0, 0))
    return pl.pallas_call(
        functools.partial(_out_kernel, tiles_per_sample=seq // tm),
        out_shape=jax.ShapeDtypeStruct((nl, D_MODEL), F32),
        grid=(nl // tm,),
        in_specs=[tile, tile, pl.BlockSpec(mod.shape, lambda i: (0, 0)),
                  pl.BlockSpec(memory_space=pltpu.VMEM), vec, vec],
        out_specs=tile,
        compiler_params=_params(("arbitrary",)),
        name="out_proj",
    )(y, x2, mod, w_out, ln_w.reshape(1, -1), ln_b.reshape(1, -1))


def _rope_tables(seq):
    t = jnp.arange(seq)
    inv = ROPE_THETA ** (-jnp.arange(0, ROT_HALF, 2, dtype=F32) / ROT_HALF)
    ang_r = (t // GRID_W).astype(F32)[:, None] * inv[None]
    ang_c = (t % GRID_W).astype(F32)[:, None] * inv[None]
    cr, sr, cc, sc = jnp.cos(ang_r), jnp.sin(ang_r), jnp.cos(ang_c), jnp.sin(ang_c)
    z = jnp.zeros_like(sr)
    cos = jnp.concatenate([cr, cr, cc, cc], axis=1)
    sin_a = jnp.concatenate([-sr, z, -sc, z], axis=1)
    sin_b = jnp.concatenate([z, sr, z, sc], axis=1)
    return cos, sin_a, sin_b


def _block_out(x, c, ctx, c_ctx, w_mod, b_mod, w_in, b_if, conv_w, conv_b, mh_norm_w,
               q_norm_w, k_norm_w, w_ba, w_bb):
    batch, seq, d = x.shape
    assert d == D_MODEL and ctx.shape == (batch, TOK, D_MODEL) and seq % TOK == 0 and seq % GRID_W == 0
    assert w_in.shape == (D_MODEL, N_IN)
    bpb = seq // TOK + 1

    rows = -(-(batch + 1) // 8) * 8
    cc = jnp.zeros((rows, D_MODEL), F32).at[:batch].set(c).at[batch].set(c_ctx)
    mod = _modulation(cc, w_mod, b_mod)

    xc = jnp.concatenate([ctx, x], axis=1).reshape(batch * bpb * TOK, D_MODEL)
    w_bf = w_in.astype(BF16)
    w_p = (w_bf[:, :W_IF], w_bf[:, W_OA:], w_bf[:, W_KB:W_OA])
    w_if = jnp.pad(w_in[:, W_IF:W_KB], ((0, 0), (0, LANES - N_GATES))).astype(BF16)
    b_if_p = jnp.pad(b_if, (0, LANES - N_GATES)).reshape(1, LANES)
    p, gates = _in_proj(xc, mod, w_p, w_if, b_if_p, bpb, batch)

    rope = _rope_tables(seq)
    qk = _qk_conv(p, conv_w, conv_b, bpb)
    k_all, v_all = _kv_prep(p, rope, k_norm_w, batch, bpb)
    h_bwd = _mlstm_sweep(qk, p, gates, batch, bpb, reverse=True)
    hn = _mlstm_sweep(qk, p, gates, batch, bpb, reverse=False, h_bwd=h_bwd, norm_w=mh_norm_w)
    o_attn = _attention(p, k_all, v_all, rope, q_norm_w, batch, bpb)
    y = _branches(hn, o_attn, p, w_ba.astype(BF16), w_bb.astype(BF16), bpb)
    return y, mod


def kernel(x, c, ctx, c_ctx, w_mod, b_mod, w_in, b_if, conv_w, conv_b, mh_norm_w, q_norm_w, k_norm_w,
           w_branch_a, w_branch_b, w_out, ln_w, ln_b):
    assert w_mod.shape[0] == DEPTH
    batch, seq, _ = x.shape
    y, mod = _block_out(x, c, ctx, c_ctx, w_mod[0], b_mod[0], w_in[0], b_if[0], conv_w[0], conv_b[0],
                        mh_norm_w[0], q_norm_w[0], k_norm_w[0], w_branch_a[0], w_branch_b[0])
    out = _out_proj(y, x.reshape(batch * seq, D_MODEL), mod, w_out[0].astype(BF16), ln_w[0], ln_b[0], seq)
    return out.reshape(batch, seq, D_MODEL)
```
